```python
import math
import jax, jax.numpy as jnp
from jax import lax
import numpy as np

D_MODEL = 1024
BATCH = 2
SEQ = 8192
DEPTH = 4

SSM_GROUP = 16
N_GROUPS = D_MODEL // SSM_GROUP
SSM_STATE = 64
N_HEADS = 16
HEAD_DIM = D_MODEL // N_HEADS
ATTN_DIM = N_HEADS * HEAD_DIM
D_FF = 4 * D_MODEL
Q_BLOCK = 128
N_A_LAYERS = DEPTH // 2
N_B_LAYERS = DEPTH - N_A_LAYERS
RMS_EPS = 1e-6
DT_MIN = 1e-3
DT_MAX = 1e-1

kernel_name = "s5_fox_yoco_hybrid_trunk"


def rmsnorm(x, g):
    xf = x.astype(jnp.float32)
    y = xf * lax.rsqrt(jnp.mean(xf * xf, axis=-1, keepdims=True) + RMS_EPS)
    return (y * g.astype(jnp.float32)).astype(x.dtype)


def sqrelu_mlp(h, w1, w2):
    a = jnp.square(jax.nn.relu(h @ w1))
    return a @ w2


def _ssm_binop(e1, e2):
    a1, b1 = e1
    a2, b2 = e2
    return a2 * a1, a2 * b1 + b2


def s5_mixer(u, log_dt, a_re, a_im, b_re, b_im, c_re, c_im, d_skip, w_glu):
    f32 = jnp.float32
    bsz, length, _ = u.shape
    uf = u.astype(f32).reshape(bsz, length, N_GROUPS, SSM_GROUP)
    lam = lax.complex(a_re.astype(f32), a_im.astype(f32))
    dt = jnp.exp(log_dt.astype(f32))[:, None]
    lam_bar = jnp.exp(lam * dt)
    b = lax.complex(b_re.astype(f32), b_im.astype(f32))
    b_bar = ((lam_bar - 1.0) / lam)[..., None] * b
    bu = jnp.einsum('blgc,gpc->blgp', uf.astype(jnp.complex64), b_bar)
    a_elems = jnp.broadcast_to(lam_bar, bu.shape)
    _, states = lax.associative_scan(_ssm_binop, (a_elems, bu), axis=1)
    c = lax.complex(c_re.astype(f32), c_im.astype(f32))
    y = jnp.real(jnp.einsum('blgp,gcp->blgc', states, c))
    y = y + d_skip.astype(f32).reshape(N_GROUPS, SSM_GROUP) * uf
    z = jax.nn.gelu(y.reshape(bsz, length, D_MODEL)).astype(u.dtype)
    zw = z @ w_glu
    val, gate = zw[..., :D_MODEL], zw[..., D_MODEL:]
    return val * jax.nn.sigmoid(gate)


def shared_kv(h, kv_norm, w_kvf, b_f):
    bsz, length, _ = h.shape
    kvf = rmsnorm(h, kv_norm) @ w_kvf
    k = kvf[..., :ATTN_DIM].reshape(bsz, length, N_HEADS, HEAD_DIM).transpose(0, 2, 1, 3)
    v = kvf[..., ATTN_DIM:2 * ATTN_DIM].reshape(bsz, length, N_HEADS, HEAD_DIM).transpose(0, 2, 1, 3)
    f_logit = kvf[..., 2 * ATTN_DIM:].astype(jnp.float32) + b_f.astype(jnp.float32)
    log_f = jax.nn.log_sigmoid(f_logit)
    cum_log_f = jnp.cumsum(log_f, axis=1).transpose(0, 2, 1)
    return k, v, cum_log_f


def fox_attention(hn, wq, wo, k, v, cum_log_f):
    bsz, length, _ = hn.shape
    nb = length // Q_BLOCK
    q = (hn @ wq) * (HEAD_DIM ** -0.5)
    q_blocks = q.reshape(bsz, nb, Q_BLOCK, N_HEADS, HEAD_DIM).transpose(1, 0, 3, 2, 4)
    f_blocks = cum_log_f.reshape(bsz, N_HEADS, nb, Q_BLOCK).transpose(2, 0, 1, 3)
    pos_q = jnp.arange(length, dtype=jnp.int32).reshape(nb, Q_BLOCK)
    pos_k = jnp.arange(length, dtype=jnp.int32)

    def one_block(args):
        qb, fq, pq = args
        s = jnp.einsum('bhqd,bhkd->bhqk', qb, k).astype(jnp.float32)
        s = s + fq[..., None] - cum_log_f[:, :, None, :]
        mask = pq[:, None] >= pos_k[None, :]
        s = jnp.where(mask[None, None], s, -jnp.inf)
        p = jax.nn.softmax(s, axis=-1)
        return jnp.einsum('bhqk,bhkd->bhqd', p.astype(v.dtype), v)

    o = lax.map(one_block, (q_blocks, f_blocks, pos_q))
    o = o.transpose(1, 0, 3, 2, 4).reshape(bsz, length, ATTN_DIM)
    return o @ wo


def setup_inputs(seed: int = 0) -> dict:
    key = jax.random.key(seed)
    ks = jax.random.split(key, 20)
    f32 = jnp.float32

    def nrm(k, shape, scale):
        return scale * jax.random.normal(k, shape, f32)

    x = nrm(ks[0], (BATCH, SEQ, D_MODEL), 1.0)
    mix_norm = 1.0 + nrm(ks[1], (DEPTH, D_MODEL), 0.05)
    mlp_norm = 1.0 + nrm(ks[2], (DEPTH, D_MODEL), 0.05)
    mlp_w1 = nrm(ks[3], (DEPTH, D_MODEL, D_FF), D_MODEL ** -0.5)
    mlp_w2 = nrm(ks[4], (DEPTH, D_FF, D_MODEL), 0.5 * D_FF ** -0.5)
    ssm_log_dt = jax.random.uniform(ks[5], (N_A_LAYERS, N_GROUPS), f32,
                                    math.log(DT_MIN), math.log(DT_MAX))
    ssm_a_re = -0.5 + nrm(ks[6], (N_A_LAYERS, N_GROUPS, SSM_STATE), 0.01)
    ssm_a_im = math.pi * jnp.arange(SSM_STATE, dtype=f32) + nrm(ks[7], (N_A_LAYERS, N_GROUPS, SSM_STATE), 0.01)
    ssm_b_re = nrm(ks[8], (N_A_LAYERS, N_GROUPS, SSM_STATE, SSM_GROUP), (2 * SSM_GROUP) ** -0.5)
    ssm_b_im = nrm(ks[9], (N_A_LAYERS, N_GROUPS, SSM_STATE, SSM_GROUP), (2 * SSM_GROUP) ** -0.5)
    ssm_c_re = nrm(ks[10], (N_A_LAYERS, N_GROUPS, SSM_GROUP, SSM_STATE), (2 * SSM_STATE) ** -0.5)
    ssm_c_im = nrm(ks[11], (N_A_LAYERS, N_GROUPS, SSM_GROUP, SSM_STATE), (2 * SSM_STATE) ** -0.5)
    ssm_d = nrm(ks[12], (N_A_LAYERS, D_MODEL), 1.0)
    ssm_w_glu = nrm(ks[13], (N_A_LAYERS, D_MODEL, 2 * D_MODEL), D_MODEL ** -0.5)
    kv_norm = 1.0 + nrm(ks[14], (D_MODEL,), 0.05)
    w_kvf = nrm(ks[15], (D_MODEL, 2 * ATTN_DIM + N_HEADS), D_MODEL ** -0.5)
    b_f = jax.random.uniform(ks[16], (N_HEADS,), f32, 0.5, 3.0)
    attn_wq = nrm(ks[17], (N_B_LAYERS, D_MODEL, ATTN_DIM), D_MODEL ** -0.5)
    attn_wo = nrm(ks[18], (N_B_LAYERS, ATTN_DIM, D_MODEL), ATTN_DIM ** -0.5)
    final_norm = 1.0 + nrm(ks[19], (D_MODEL,), 0.05)
    return {"x": x, "mix_norm": mix_norm, "mlp_norm": mlp_norm, "mlp_w1": mlp_w1, "mlp_w2": mlp_w2,
            "ssm_log_dt": ssm_log_dt, "ssm_a_re": ssm_a_re, "ssm_a_im": ssm_a_im,
            "ssm_b_re": ssm_b_re, "ssm_b_im": ssm_b_im, "ssm_c_re": ssm_c_re, "ssm_c_im": ssm_c_im,
            "ssm_d": ssm_d, "ssm_w_glu": ssm_w_glu, "kv_norm": kv_norm, "w_kvf": w_kvf, "b_f": b_f,
            "attn_wq": attn_wq, "attn_wo": attn_wo, "final_norm": final_norm}


def reference(x, mix_norm, mlp_norm, mlp_w1, mlp_w2, ssm_log_dt, ssm_a_re, ssm_a_im,
              ssm_b_re, ssm_b_im, ssm_c_re, ssm_c_im, ssm_d, ssm_w_glu, kv_norm, w_kvf, b_f,
              attn_wq, attn_wo, final_norm):
    h = x
    k = v = cum_log_f = None
    for i in range(DEPTH):
        hn = rmsnorm(h, mix_norm[i])
        if i < N_A_LAYERS:
            h = h + s5_mixer(hn, ssm_log_dt[i], ssm_a_re[i], ssm_a_im[i], ssm_b_re[i], ssm_b_im[i],
                             ssm_c_re[i], ssm_c_im[i], ssm_d[i], ssm_w_glu[i])
        else:
            j = i - N_A_LAYERS
            h = h + fox_attention(hn, attn_wq[j], attn_wo[j], k, v, cum_log_f)
        h = h + sqrelu_mlp(rmsnorm(h, mlp_norm[i]), mlp_w1[i], mlp_w2[i])
        if i == N_A_LAYERS - 1:
            k, v, cum_log_f = shared_kv(h, kv_norm, w_kvf, b_f)
    return rmsnorm(h, final_norm)
```

```python
import functools
import math

import numpy as np
import jax
import jax.numpy as jnp
from jax import lax
from jax.experimental import pallas as pl
from jax.experimental.pallas import tpu as pltpu

D_MODEL = 1024
N_GROUPS = 64
SSM_GROUP = 16
GROUP_SHIFT = 4
SSM_STATE = 64
N_HEADS = 16
HEAD_DIM = 64
D_FF = 4 * D_MODEL
RMS_EPS = 1e-6
N_A_LAYERS = 2
DEPTH = 4

SSM_CHUNK = 64
CHUNK_W = SSM_CHUNK * SSM_GROUP
LANES = 128
HEAD_PAD = 128
N_EXTRA = 3

ROW_TILE = 512
FF_CHUNK = 1024
ATT_BQ = 256
ATT_BK = 512
NEG_BIG = -1e30

VMEM_LIMIT = 56 * 1024 * 1024

F32 = jnp.float32
BF16 = jnp.bfloat16


def _cparams(sem):
    return pltpu.CompilerParams(dimension_semantics=sem, vmem_limit_bytes=VMEM_LIMIT)


def _rms(x, g):
    return x * lax.rsqrt(jnp.mean(x * x, axis=-1, keepdims=True) + RMS_EPS) * g


def _dot(a, b):
    return jnp.dot(a, b, preferred_element_type=F32)


def _dot_nt(a, b, precision=None):
    return lax.dot_general(a, b, (((1,), (1,)), ((), ())),
                           preferred_element_type=F32, precision=precision)


def _split3(x):
    hi = x.astype(BF16)
    r1 = x - hi.astype(F32)
    mid = r1.astype(BF16)
    lo = (r1 - mid.astype(F32)).astype(BF16)
    return hi, mid, lo


def _rmsnorm_kernel(h_ref, g_ref, o_ref):
    o_ref[...] = _rms(h_ref[...], g_ref[...])


def _rmsnorm(h2, g):
    m = h2.shape[0]
    return pl.pallas_call(
        _rmsnorm_kernel,
        grid=(m // ROW_TILE,),
        in_specs=[pl.BlockSpec((ROW_TILE, D_MODEL), lambda i: (i, 0)),
                  pl.BlockSpec((1, D_MODEL), lambda i: (0, 0))],
        out_specs=pl.BlockSpec((ROW_TILE, D_MODEL), lambda i: (i, 0)),
        out_shape=jax.ShapeDtypeStruct((m, D_MODEL), F32),
        compiler_params=_cparams(("parallel",)),
        name="rmsnorm",
    )(h2, g.reshape(1, D_MODEL))


def _swap_halves(x):
    return pltpu.roll(x, 64, axis=x.ndim - 1)


def _ssm_prep_kernel(ldt_ref, are_ref, aim_ref, bt_ref, cc_ref,
                     w2_ref, pt_ref, r_ref, a1_ref, a2_ref):
    lane1 = lax.broadcasted_iota(jnp.int32, (1, LANES), 1)
    lo1 = lane1 < SSM_STATE
    a_re = are_ref[0]
    a_im = aim_ref[0]
    dt = jnp.exp(ldt_ref[0])
    zr = a_re * dt
    zi = a_im * dt
    mag = jnp.exp(zr)
    lr = mag * jnp.cos(zi)
    li = mag * jnp.sin(zi)
    imag = jnp.exp(-zr)
    ir = imag * jnp.cos(zi)
    ii = -imag * jnp.sin(zi)

    def cmul_s(x, wr, wi):
        return x * wr + _swap_halves(x) * jnp.where(lo1, -wi, wi)

    def cmul_g(x, y):
        ys = _swap_halves(y)
        lo = lax.broadcasted_iota(jnp.int32, y.shape, 1) < SSM_STATE
        return x * jnp.where(lo, y, ys) + _swap_halves(x) * jnp.where(lo, -ys, y)

    def powers(n, nbits, wr, wi):
        p = jnp.broadcast_to(jnp.where(lo1, 1.0, 0.0).astype(F32), (n.shape[0], LANES))
        for k in range(nbits):
            bit = ((n >> k) & 1) == 1
            p = jnp.where(bit, cmul_s(p, wr, wi), p)
            wr, wi = wr * wr - wi * wi, 2.0 * wr * wi
        return p

    nr = lr - 1.0
    den = a_re * a_re + a_im * a_im
    cr = (nr * a_re + li * a_im) / den
    ci = (li * a_re - nr * a_im) / den
    bbar = cmul_s(bt_ref[0], cr, ci)
    cc = cc_ref[0]

    def tile_rows(x, reps):
        return jnp.concatenate([x] * reps, axis=0)

    row = lax.broadcasted_iota(jnp.int32, (CHUNK_W, 1), 0)
    n_big = row >> GROUP_SHIFT
    e_pos = powers(n_big, 6, lr, li)
    cc_t = tile_rows(cc, SSM_CHUNK)
    bb_t = tile_rows(bbar, SSM_CHUNK)
    lo_big = lax.broadcasted_iota(jnp.int32, (CHUNK_W, LANES), 1) < SSM_STATE

    cbig = cmul_g(e_pos, cc_t)
    cl1 = cmul_s(cbig, lr, li)
    pt_ref[0] = jnp.where(lo_big, cl1, -cl1).astype(BF16)
    e_rev = powers(SSM_CHUNK - 1 - n_big, 6, lr, li)
    r_ref[0] = cmul_g(e_rev, bb_t).astype(BF16)

    row8 = lax.broadcasted_iota(jnp.int32, (LANES, 1), 0)
    n8 = row8 >> GROUP_SHIFT
    bsmall = cmul_g(powers(n8, 3, ir, ii), tile_rows(bbar, LANES // SSM_GROUP))
    lo8 = lax.broadcasted_iota(jnp.int32, (LANES, LANES), 1) < SSM_STATE
    lhs = jnp.where(lo8, bsmall, -bsmall)
    w2 = _dot_nt(lhs, cbig, precision=lax.Precision.HIGHEST)
    col_t = lax.broadcasted_iota(jnp.int32, (LANES, CHUNK_W), 1) >> GROUP_SHIFT
    w2_ref[0] = jnp.where(col_t >= n8, w2, 0.0).astype(BF16)

    ar, ai = lr, li
    for _ in range(6):
        ar, ai = ar * ar - ai * ai, 2.0 * ar * ai
    a1_ref[0] = ar
    a2_ref[0] = jnp.where(lo1, -ai, ai)


def _ssm_prep(log_dt, a_re, a_im, b_re, b_im, c_re, c_im):
    g = N_GROUPS
    dup = lambda x: jnp.concatenate([x, x], axis=-1).reshape(g, 1, LANES)
    ldt = jnp.broadcast_to(log_dt.reshape(g, 1, 1), (g, 1, LANES))
    bt = jnp.concatenate([b_re.transpose(0, 2, 1), b_im.transpose(0, 2, 1)], axis=-1)
    cc = jnp.concatenate([c_re, c_im], axis=-1)
    vec = pl.BlockSpec((1, 1, LANES), lambda i: (i, 0, 0))
    mat = pl.BlockSpec((1, SSM_GROUP, LANES), lambda i: (i, 0, 0))
    return pl.pallas_call(
        _ssm_prep_kernel,
        grid=(g,),
        in_specs=[vec, vec, vec, mat, mat],
        out_specs=[pl.BlockSpec((1, LANES, CHUNK_W), lambda i: (i, 0, 0)),
                   pl.BlockSpec((1, CHUNK_W, LANES), lambda i: (i, 0, 0)),
                   pl.BlockSpec((1, CHUNK_W, LANES), lambda i: (i, 0, 0)),
                   vec, vec],
        out_shape=[jax.ShapeDtypeStruct((g, LANES, CHUNK_W), BF16),
                   jax.ShapeDtypeStruct((g, CHUNK_W, LANES), BF16),
                   jax.ShapeDtypeStruct((g, CHUNK_W, LANES), BF16),
                   jax.ShapeDtypeStruct((g, 1, LANES), F32),
                   jax.ShapeDtypeStruct((g, 1, LANES), F32)],
        compiler_params=_cparams(("parallel",)),
        name="ssm_prep",
    )(ldt, dup(a_re), dup(a_im), bt, cc)


def _ssm_summary_kernel(u_ref, r_ref, v_ref):
    v_ref[...] = _dot(u_ref[0].astype(BF16), r_ref[0])


def _ssm_summary(u, r):
    g, m, _ = u.shape
    return pl.pallas_call(
        _ssm_summary_kernel,
        grid=(g,),
        in_specs=[pl.BlockSpec((1, m, CHUNK_W), lambda i: (i, 0, 0)),
                  pl.BlockSpec((1, CHUNK_W, LANES), lambda i: (i, 0, 0))],
        out_specs=pl.BlockSpec((m, LANES), lambda i: (0, i)),
        out_shape=jax.ShapeDtypeStruct((m, g * LANES), F32),
        compiler_params=_cparams(("parallel",)),
        name="ssm_summary",
    )(u, r)


def _ssm_scan_kernel(v_ref, a1_ref, a2_ref, s_ref):
    a1 = a1_ref[...]
    a2 = a2_ref[...]
    nk = v_ref.shape[1]

    def step(k, s):
        s_ref[0, k] = s
        return s * a1 + _swap_halves(s) * a2 + v_ref[0, k]

    lax.fori_loop(0, nk, step, jnp.zeros((N_GROUPS, LANES), F32))


def _ssm_scan(v4, a1, a2):
    b, nk = v4.shape[0], v4.shape[1]
    blk = pl.BlockSpec((1, nk, N_GROUPS, LANES), lambda i: (i, 0, 0, 0))
    coef = pl.BlockSpec((N_GROUPS, LANES), lambda i: (0, 0))
    return pl.pallas_call(
        _ssm_scan_kernel,
        grid=(b,),
        in_specs=[blk, coef, coef],
        out_specs=blk,
        out_shape=jax.ShapeDtypeStruct(v4.shape, F32),
        compiler_params=_cparams(("parallel",)),
        name="ssm_scan",
    )(v4, a1, a2)


def _ssm_output_kernel(u_ref, s_ref, w2_ref, pt_ref, d_ref, z_ref, toep_ref):
    @pl.when(pl.program_id(0) == 0)
    def _():
        toep_ref[...] = jnp.zeros_like(toep_ref)

    nblk = CHUNK_W // LANES
    for i in range(nblk):
        toep_ref[i * LANES:(i + 1) * LANES, i * LANES:] = w2_ref[0, :, :CHUNK_W - i * LANES]

    u = u_ref[0]
    ub = u.astype(BF16)
    y = _dot_nt(s_ref[...].astype(BF16), pt_ref[0]) + d_ref[0] * u
    tile = 256
    cols = []
    for j in range(CHUNK_W // tile):
        kk = (j + 1) * tile
        cols.append(_dot(ub[:, :kk], toep_ref[:kk, j * tile:(j + 1) * tile]))
    y = y + jnp.concatenate(cols, axis=1)
    c0 = math.sqrt(2.0 / math.pi)
    z_ref[0] = (0.5 * y * (1.0 + jnp.tanh(c0 * (y + 0.044715 * (y * y * y))))).astype(BF16)


def _ssm_output(u, s, w2, pt, d_t):
    g, m, _ = u.shape
    return pl.pallas_call(
        _ssm_output_kernel,
        grid=(g,),
        in_specs=[pl.BlockSpec((1, m, CHUNK_W), lambda i: (i, 0, 0)),
                  pl.BlockSpec((m, LANES), lambda i: (0, i)),
                  pl.BlockSpec((1, LANES, CHUNK_W), lambda i: (i, 0, 0)),
                  pl.BlockSpec((1, CHUNK_W, LANES), lambda i: (i, 0, 0)),
                  pl.BlockSpec((1, 1, CHUNK_W), lambda i: (i, 0, 0))],
        out_specs=pl.BlockSpec((1, m, CHUNK_W), lambda i: (i, 0, 0)),
        out_shape=jax.ShapeDtypeStruct((g, m, CHUNK_W), BF16),
        scratch_shapes=[pltpu.VMEM((CHUNK_W, CHUNK_W), BF16)],
        compiler_params=_cparams(("arbitrary",)),
        name="ssm_output",
    )(u, s, w2, pt, d_t)


def _glu_kernel(h_ref, z_ref, w_ref, o_ref):
    zw = _dot(z_ref[...], w_ref[...])
    o_ref[...] = h_ref[...] + zw[:, :D_MODEL] * jax.nn.sigmoid(zw[:, D_MODEL:])


def _proj_res_kernel(h_ref, z_ref, w_ref, o_ref):
    o_ref[...] = h_ref[...] + _dot(z_ref[...], w_ref[...])


def _row_call(kernel, name, h2, z2, w):
    m = h2.shape[0]
    return pl.pallas_call(
        kernel,
        grid=(m // ROW_TILE,),
        in_specs=[pl.BlockSpec((ROW_TILE, D_MODEL), lambda i: (i, 0)),
                  pl.BlockSpec((ROW_TILE, z2.shape[1]), lambda i: (i, 0)),
                  pl.BlockSpec(w.shape, lambda i: (0, 0))],
        out_specs=pl.BlockSpec((ROW_TILE, D_MODEL), lambda i: (i, 0)),
        out_shape=jax.ShapeDtypeStruct((m, D_MODEL), F32),
        compiler_params=_cparams(("parallel",)),
        name=name,
    )(h2, z2, w)


def _mlp_kernel(h_ref, g_ref, w1_ref, w2_ref, fg_ref, o_ref, *, final_norm):
    x = h_ref[...]
    xn = _rms(x, g_ref[...]).astype(BF16)
    acc = x
    for f in range(0, D_FF, FF_CHUNK):
        a = jnp.square(jnp.maximum(_dot(xn, w1_ref[:, f:f + FF_CHUNK]), 0.0))
        acc = acc + _dot(a.astype(BF16), w2_ref[f:f + FF_CHUNK, :])
    if final_norm:
        acc = _rms(acc, fg_ref[...])
    o_ref[...] = acc


def _mlp(h2, g, w1, w2, fg, final_norm):
    m = h2.shape[0]
    return pl.pallas_call(
        functools.partial(_mlp_kernel, final_norm=final_norm),
        grid=(m // ROW_TILE,),
        in_specs=[pl.BlockSpec((ROW_TILE, D_MODEL), lambda i: (i, 0)),
                  pl.BlockSpec((1, D_MODEL), lambda i: (0, 0)),
                  pl.BlockSpec((D_MODEL, D_FF), lambda i: (0, 0)),
                  pl.BlockSpec((D_FF, D_MODEL), lambda i: (0, 0)),
                  pl.BlockSpec((1, D_MODEL), lambda i: (0, 0))],
        out_specs=pl.BlockSpec((ROW_TILE, D_MODEL), lambda i: (i, 0)),
        out_shape=jax.ShapeDtypeStruct((m, D_MODEL), F32),
        compiler_params=_cparams(("parallel",)),
        name="mlp_final" if final_norm else "mlp",
    )(h2, g.reshape(1, D_MODEL), w1, w2, fg.reshape(1, D_MODEL))


def _extra_scatter(sign_f, f_first):
    scat = np.zeros((N_EXTRA, LANES, N_HEADS * HEAD_PAD), np.float32)
    const = np.zeros((1, N_HEADS * HEAD_PAD), np.float32)
    for h in range(N_HEADS):
        base = h * HEAD_PAD + HEAD_DIM
        f0, o0 = (0, N_EXTRA) if f_first else (N_EXTRA, 0)
        for j in range(N_EXTRA):
            scat[j, h, base + f0 + j] = sign_f
            const[0, base + o0 + j] = 1.0
    return jnp.asarray(scat, BF16), jnp.asarray(const, F32)


def _kv_kernel(h_ref, g_ref, wk_ref, wv_ref, wf_ref, bf_ref, sc_ref, cst_ref,
               k_ref, v_ref, f_ref, carry_ref):
    @pl.when(pl.program_id(1) == 0)
    def _():
        carry_ref[...] = jnp.zeros_like(carry_ref)

    xn = _rms(h_ref[0], g_ref[...]).astype(BF16)
    v_ref[0] = _dot(xn, wv_ref[...]).astype(BF16)

    logit = _dot(xn, wf_ref[...]) + bf_ref[...]
    log_f = jnp.minimum(logit, 0.0) - jnp.log1p(jnp.exp(-jnp.abs(logit)))
    lane = lax.broadcasted_iota(jnp.int32, log_f.shape, 1)
    log_f = jnp.where(lane < N_HEADS, log_f, 0.0)
    t = log_f.shape[0]
    tri = (lax.broadcasted_iota(jnp.int32, (t, t), 0)
           >= lax.broadcasted_iota(jnp.int32, (t, t), 1)).astype(BF16)
    hi, mid, lo = _split3(log_f)
    cum = (_dot(tri, hi) + _dot(tri, mid)) + _dot(tri, lo) + carry_ref[...]
    carry_ref[...] = cum[t - 1:t, :]
    f_ref[0] = cum

    fh, fm, fl = _split3(cum)
    kaug = (_dot(xn, wk_ref[...]) + cst_ref[...]
            + _dot(fh, sc_ref[0]) + _dot(fm, sc_ref[1]) + _dot(fl, sc_ref[2])).astype(BF16)
    for hh in range(N_HEADS):
        k_ref[0, hh] = kaug[:, hh * HEAD_PAD:(hh + 1) * HEAD_PAD]


def _kv_proj(h3, g, wk_aug, wv, wf, bf):
    b, l, _ = h3.shape
    scat, const = _extra_scatter(-1.0, f_first=False)
    full = lambda a: pl.BlockSpec(a.shape, lambda i, j: (0,) * a.ndim)
    return pl.pallas_call(
        _kv_kernel,
        grid=(b, l // ROW_TILE),
        in_specs=[pl.BlockSpec((1, ROW_TILE, D_MODEL), lambda i, j: (i, j, 0)),
                  pl.BlockSpec((1, D_MODEL), lambda i, j: (0, 0)),
                  full(wk_aug), full(wv), full(wf), full(bf), full(scat), full(const)],
        out_specs=[pl.BlockSpec((1, N_HEADS, ROW_TILE, HEAD_PAD), lambda i, j: (i, 0, j, 0)),
                   pl.BlockSpec((1, ROW_TILE, D_MODEL), lambda i, j: (i, j, 0)),
                   pl.BlockSpec((1, ROW_TILE, LANES), lambda i, j: (i, j, 0))],
        out_shape=[jax.ShapeDtypeStruct((b, N_HEADS, l, HEAD_PAD), BF16),
                   jax.ShapeDtypeStruct((b, l, D_MODEL), BF16),
                   jax.ShapeDtypeStruct((b, l, LANES), F32)],
        scratch_shapes=[pltpu.VMEM((1, LANES), F32)],
        compiler_params=_cparams(("parallel", "arbitrary")),
        name="kv_proj",
    )(h3, g.reshape(1, D_MODEL), wk_aug, wv, wf, bf, scat, const)


def _q_kernel(h_ref, g_ref, wq_ref, f_ref, sc_ref, cst_ref, q_ref):
    xn = _rms(h_ref[0], g_ref[...]).astype(BF16)
    fh, fm, fl = _split3(f_ref[0])
    qaug = (_dot(xn, wq_ref[...]) + cst_ref[...]
            + _dot(fh, sc_ref[0]) + _dot(fm, sc_ref[1]) + _dot(fl, sc_ref[2])).astype(BF16)
    for hh in range(N_HEADS):
        q_ref[0, hh] = qaug[:, hh * HEAD_PAD:(hh + 1) * HEAD_PAD]


def _q_proj(h3, g, wq_aug, fcum):
    b, l, _ = h3.shape
    scat, const = _extra_scatter(1.0, f_first=True)
    full = lambda a: pl.BlockSpec(a.shape, lambda i, j: (0,) * a.ndim)
    return pl.pallas_call(
        _q_kernel,
        grid=(b, l // ROW_TILE),
        in_specs=[pl.BlockSpec((1, ROW_TILE, D_MODEL), lambda i, j: (i, j, 0)),
                  pl.BlockSpec((1, D_MODEL), lambda i, j: (0, 0)),
                  full(wq_aug),
                  pl.BlockSpec((1, ROW_TILE, LANES), lambda i, j: (i, j, 0)),
                  full(scat), full(const)],
        out_specs=pl.BlockSpec((1, N_HEADS, ROW_TILE, HEAD_PAD), lambda i, j: (i, 0, j, 0)),
        out_shape=jax.ShapeDtypeStruct((b, N_HEADS, l, HEAD_PAD), BF16),
        compiler_params=_cparams(("parallel", "parallel")),
        name="q_proj",
    )(h3, g.reshape(1, D_MODEL), wq_aug, fcum, scat, const)


def _attn_kernel(q_ref, k_ref, v_ref, o_ref):
    qi = pl.program_id(2)
    n_full = qi * ATT_BQ // ATT_BK
    lane = lax.broadcasted_iota(jnp.int32, (ATT_BQ, LANES), 1)

    def block(j, carry, masked):
        out = []
        for hh in range(2):
            m, l, acc = carry[hh]
            kblk = k_ref[0, hh, pl.ds(j * ATT_BK, ATT_BK), :]
            s = _dot_nt(q_ref[0, hh], kblk)
            if masked:
                qpos = qi * ATT_BQ + lax.broadcasted_iota(jnp.int32, s.shape, 0)
                kpos = j * ATT_BK + lax.broadcasted_iota(jnp.int32, s.shape, 1)
                s = jnp.where(qpos >= kpos, s, NEG_BIG)
            m_new = jnp.maximum(m, jnp.max(s, axis=-1, keepdims=True))
            p = jnp.exp(s - m_new)
            alpha = jnp.exp(m - m_new)
            l = alpha * l + jnp.sum(p, axis=-1, keepdims=True)
            vblk = v_ref[0, pl.ds(j * ATT_BK, ATT_BK), :]
            acc = alpha * acc + _dot(p.astype(BF16), vblk)
            out.append((m_new, l, acc))
        return tuple(out)

    init = tuple((jnp.full((ATT_BQ, 1), NEG_BIG, F32), jnp.zeros((ATT_BQ, 1), F32),
                  jnp.zeros((ATT_BQ, LANES), F32)) for _ in range(2))
    carry = lax.fori_loop(0, n_full, lambda j, c: block(j, c, False), init)
    carry = block(n_full, carry, True)
    (_, l0, acc0), (_, l1, acc1) = carry
    o_ref[0] = jnp.where(lane < HEAD_DIM, acc0 / l0, acc1 / l1).astype(BF16)


def _attention(q_aug, k_aug, v):
    b, _, l, _ = q_aug.shape
    return pl.pallas_call(
        _attn_kernel,
        grid=(b, N_HEADS // 2, l // ATT_BQ),
        in_specs=[pl.BlockSpec((1, 2, ATT_BQ, HEAD_PAD), lambda i, h, q: (i, h, q, 0)),
                  pl.BlockSpec((1, 2, l, HEAD_PAD), lambda i, h, q: (i, h, 0, 0)),
                  pl.BlockSpec((1, l, 2 * HEAD_DIM), lambda i, h, q: (i, 0, h))],
        out_specs=pl.BlockSpec((1, ATT_BQ, 2 * HEAD_DIM), lambda i, h, q: (i, q, h)),
        out_shape=jax.ShapeDtypeStruct((b, l, D_MODEL), BF16),
        compiler_params=_cparams(("parallel", "parallel", "arbitrary")),
        name="fox_attention",
    )(q_aug, k_aug, v)


def _pad_heads(w):
    w3 = w.reshape(D_MODEL, N_HEADS, HEAD_DIM)
    w3 = jnp.pad(w3, ((0, 0), (0, 0), (0, HEAD_PAD - HEAD_DIM)))
    return w3.reshape(D_MODEL, N_HEADS * HEAD_PAD)


def kernel(x, mix_norm, mlp_norm, mlp_w1, mlp_w2, ssm_log_dt, ssm_a_re, ssm_a_im,
           ssm_b_re, ssm_b_im, ssm_c_re, ssm_c_im, ssm_d, ssm_w_glu, kv_norm, w_kvf, b_f,
           attn_wq, attn_wo, final_norm):
    bsz, length, _ = x.shape
    m = bsz * length
    nk = length // SSM_CHUNK
    h = x.reshape(m, D_MODEL)

    for i in range(DEPTH):
        if i < N_A_LAYERS:
            w2, pt, r, a1, a2 = _ssm_prep(ssm_log_dt[i], ssm_a_re[i], ssm_a_im[i], ssm_b_re[i],
                                          ssm_b_im[i], ssm_c_re[i], ssm_c_im[i])
            hn = _rmsnorm(h, mix_norm[i])
            u = hn.reshape(bsz * nk, SSM_CHUNK, N_GROUPS, SSM_GROUP).transpose(2, 0, 1, 3)
            u = u.reshape(N_GROUPS, bsz * nk, CHUNK_W)
            v = _ssm_summary(u, r)
            s = _ssm_scan(v.reshape(bsz, nk, N_GROUPS, LANES),
                          a1.reshape(N_GROUPS, LANES), a2.reshape(N_GROUPS, LANES))
            d_t = jnp.tile(ssm_d[i].reshape(N_GROUPS, 1, SSM_GROUP), (1, 1, SSM_CHUNK))
            z = _ssm_output(u, s.reshape(bsz * nk, N_GROUPS * LANES), w2, pt, d_t)
            z = z.reshape(N_GROUPS, bsz * nk, SSM_CHUNK, SSM_GROUP).transpose(1, 2, 0, 3)
            h = _row_call(_glu_kernel, "glu", h, z.reshape(m, D_MODEL), ssm_w_glu[i].astype(BF16))
        else:
            j = i - N_A_LAYERS
            wq_aug = _pad_heads(attn_wq[j] * (HEAD_DIM ** -0.5)).astype(BF16)
            q_aug = _q_proj(h.reshape(bsz, length, D_MODEL), mix_norm[i], wq_aug, fcum)
            o = _attention(q_aug, k_aug, v_nat)
            h = _row_call(_proj_res_kernel, "attn_out", h, o.reshape(m, D_MODEL),
                          attn_wo[j].astype(BF16))
        h = _mlp(h, mlp_norm[i], mlp_w1[i].astype(BF16), mlp_w2[i].astype(BF16),
                 final_norm, final_norm=(i == DEPTH - 1))
        if i == N_A_LAYERS - 1:
            attn_dim = N_HEADS * HEAD_DIM
            wk_aug = _pad_heads(w_kvf[:, :attn_dim]).astype(BF16)
            wv = w_kvf[:, attn_dim:2 * attn_dim].astype(BF16)
            wf = jnp.pad(w_kvf[:, 2 * attn_dim:], ((0, 0), (0, LANES - N_HEADS))).astype(BF16)
            bf = jnp.pad(b_f, (0, LANES - N_HEADS)).reshape(1, LANES)
            k_aug, v_nat, fcum = _kv_proj(h.reshape(bsz, length, D_MODEL), kv_norm,
                                          wk_aug, wv, wf, bf)
    return h.reshape(bsz, length, D_MODEL)
```

```python
import functools
import math

import numpy as np
import jax
import jax.numpy as jnp
from jax import lax
from jax.experimental import pallas as pl
from jax.experimental.pallas import tpu as pltpu

D_MODEL = 1024
N_GROUPS = 64
SSM_GROUP = 16
GROUP_SHIFT = 4
SSM_STATE = 64
N_HEADS = 16
HEAD_DIM = 64
D_FF = 4 * D_MODEL
RMS_EPS = 1e-6
N_A_LAYERS = 2
DEPTH = 4

SSM_CHUNK = 64
CHUNK_W = SSM_CHUNK * SSM_GROUP
LANES = 128
HEAD_PAD = 128
N_EXTRA = 3

ROW_TILE = 512
FF_CHUNK = 1024
ATT_BQ = 256
ATT_BK = 512
NEG_BIG = -1e30
NULL_KEY = 3.0 * NEG_BIG
LOG2E = math.log2(math.e)

assert ROW_TILE == ATT_BK and ATT_BK % ATT_BQ == 0

VMEM_LIMIT = 56 * 1024 * 1024

F32 = jnp.float32
BF16 = jnp.bfloat16


def _cparams(sem):
    return pltpu.CompilerParams(dimension_semantics=sem, vmem_limit_bytes=VMEM_LIMIT)


def _rms(x, g):
    return x * lax.rsqrt(jnp.mean(x * x, axis=-1, keepdims=True) + RMS_EPS) * g


def _dot(a, b):
    return jnp.dot(a, b, preferred_element_type=F32)


def _dot_nt(a, b, precision=None):
    return lax.dot_general(a, b, (((1,), (1,)), ((), ())),
                           preferred_element_type=F32, precision=precision)


def _split3(x):
    hi = x.astype(BF16)
    r1 = x - hi.astype(F32)
    mid = r1.astype(BF16)
    lo = (r1 - mid.astype(F32)).astype(BF16)
    return hi, mid, lo


def _rmsnorm_kernel(h_ref, g_ref, o_ref):
    o_ref[...] = _rms(h_ref[...], g_ref[...])


def _rmsnorm(h2, g):
    m = h2.shape[0]
    return pl.pallas_call(
        _rmsnorm_kernel,
        grid=(m // ROW_TILE,),
        in_specs=[pl.BlockSpec((ROW_TILE, D_MODEL), lambda i: (i, 0)),
                  pl.BlockSpec((1, D_MODEL), lambda i: (0, 0))],
        out_specs=pl.BlockSpec((ROW_TILE, D_MODEL), lambda i: (i, 0)),
        out_shape=jax.ShapeDtypeStruct((m, D_MODEL), F32),
        compiler_params=_cparams(("parallel",)),
        name="rmsnorm",
    )(h2, g.reshape(1, D_MODEL))


def _swap_halves(x):
    return pltpu.roll(x, 64, axis=x.ndim - 1)


def _ssm_prep_kernel(ldt_ref, are_ref, aim_ref, bt_ref, cc_ref,
                     w2_ref, pt_ref, r_ref, a1_ref, a2_ref):
    lane1 = lax.broadcasted_iota(jnp.int32, (1, LANES), 1)
    lo1 = lane1 < SSM_STATE
    a_re = are_ref[0]
    a_im = aim_ref[0]
    dt = jnp.exp(ldt_ref[0])
    zr = a_re * dt
    zi = a_im * dt
    mag = jnp.exp(zr)
    lr = mag * jnp.cos(zi)
    li = mag * jnp.sin(zi)
    imag = jnp.exp(-zr)
    ir = imag * jnp.cos(zi)
    ii = -imag * jnp.sin(zi)

    def cmul_s(x, wr, wi):
        return x * wr + _swap_halves(x) * jnp.where(lo1, -wi, wi)

    def cmul_g(x, y):
        ys = _swap_halves(y)
        lo = lax.broadcasted_iota(jnp.int32, y.shape, 1) < SSM_STATE
        return x * jnp.where(lo, y, ys) + _swap_halves(x) * jnp.where(lo, -ys, y)

    def powers(n, nbits, wr, wi):
        p = jnp.broadcast_to(jnp.where(lo1, 1.0, 0.0).astype(F32), (n.shape[0], LANES))
        for k in range(nbits):
            bit = ((n >> k) & 1) == 1
            p = jnp.where(bit, cmul_s(p, wr, wi), p)
            wr, wi = wr * wr - wi * wi, 2.0 * wr * wi
        return p

    nr = lr - 1.0
    den = a_re * a_re + a_im * a_im
    cr = (nr * a_re + li * a_im) / den
    ci = (li * a_re - nr * a_im) / den
    bbar = cmul_s(bt_ref[0], cr, ci)
    cc = cc_ref[0]

    def tile_rows(x, reps):
        return jnp.concatenate([x] * reps, axis=0)

    row = lax.broadcasted_iota(jnp.int32, (CHUNK_W, 1), 0)
    n_big = row >> GROUP_SHIFT
    e_pos = powers(n_big, 6, lr, li)
    cc_t = tile_rows(cc, SSM_CHUNK)
    bb_t = tile_rows(bbar, SSM_CHUNK)
    lo_big = lax.broadcasted_iota(jnp.int32, (CHUNK_W, LANES), 1) < SSM_STATE

    cbig = cmul_g(e_pos, cc_t)
    cl1 = cmul_s(cbig, lr, li)
    pt_ref[0] = jnp.where(lo_big, cl1, -cl1).astype(BF16)
    e_rev = powers(SSM_CHUNK - 1 - n_big, 6, lr, li)
    r_ref[0] = cmul_g(e_rev, bb_t).astype(BF16)

    row8 = lax.broadcasted_iota(jnp.int32, (LANES, 1), 0)
    n8 = row8 >> GROUP_SHIFT
    bsmall = cmul_g(powers(n8, 3, ir, ii), tile_rows(bbar, LANES // SSM_GROUP))
    lo8 = lax.broadcasted_iota(jnp.int32, (LANES, LANES), 1) < SSM_STATE
    lhs = jnp.where(lo8, bsmall, -bsmall)
    w2 = _dot_nt(lhs, cbig, precision=lax.Precision.HIGHEST)
    col_t = lax.broadcasted_iota(jnp.int32, (LANES, CHUNK_W), 1) >> GROUP_SHIFT
    w2_ref[0] = jnp.where(col_t >= n8, w2, 0.0).astype(BF16)

    ar, ai = lr, li
    for _ in range(6):
        ar, ai = ar * ar - ai * ai, 2.0 * ar * ai
    a1_ref[0] = ar
    a2_ref[0] = jnp.where(lo1, -ai, ai)


def _ssm_prep(log_dt, a_re, a_im, b_re, b_im, c_re, c_im):
    g = N_GROUPS
    dup = lambda x: jnp.concatenate([x, x], axis=-1).reshape(g, 1, LANES)
    ldt = jnp.broadcast_to(log_dt.reshape(g, 1, 1), (g, 1, LANES))
    bt = jnp.concatenate([b_re.transpose(0, 2, 1), b_im.transpose(0, 2, 1)], axis=-1)
    cc = jnp.concatenate([c_re, c_im], axis=-1)
    vec = pl.BlockSpec((1, 1, LANES), lambda i: (i, 0, 0))
    mat = pl.BlockSpec((1, SSM_GROUP, LANES), lambda i: (i, 0, 0))
    return pl.pallas_call(
        _ssm_prep_kernel,
        grid=(g,),
        in_specs=[vec, vec, vec, mat, mat],
        out_specs=[pl.BlockSpec((1, LANES, CHUNK_W), lambda i: (i, 0, 0)),
                   pl.BlockSpec((1, CHUNK_W, LANES), lambda i: (i, 0, 0)),
                   pl.BlockSpec((1, CHUNK_W, LANES), lambda i: (i, 0, 0)),
                   vec, vec],
        out_shape=[jax.ShapeDtypeStruct((g, LANES, CHUNK_W), BF16),
                   jax.ShapeDtypeStruct((g, CHUNK_W, LANES), BF16),
                   jax.ShapeDtypeStruct((g, CHUNK_W, LANES), BF16),
                   jax.ShapeDtypeStruct((g, 1, LANES), F32),
                   jax.ShapeDtypeStruct((g, 1, LANES), F32)],
        compiler_params=_cparams(("parallel",)),
        name="ssm_prep",
    )(ldt, dup(a_re), dup(a_im), bt, cc)


def _ssm_summary_kernel(u_ref, r_ref, v_ref):
    v_ref[...] = _dot(u_ref[0].astype(BF16), r_ref[0])


def _ssm_summary(u, r):
    g, m, _ = u.shape
    return pl.pallas_call(
        _ssm_summary_kernel,
        grid=(g,),
        in_specs=[pl.BlockSpec((1, m, CHUNK_W), lambda i: (i, 0, 0)),
                  pl.BlockSpec((1, CHUNK_W, LANES), lambda i: (i, 0, 0))],
        out_specs=pl.BlockSpec((m, LANES), lambda i: (0, i)),
        out_shape=jax.ShapeDtypeStruct((m, g * LANES), F32),
        compiler_params=_cparams(("parallel",)),
        name="ssm_summary",
    )(u, r)


def _ssm_scan_kernel(v_ref, a1_ref, a2_ref, s_ref):
    a1 = a1_ref[...]
    a2 = a2_ref[...]
    nk = v_ref.shape[1]

    def step(k, s):
        s_ref[0, k] = s
        return s * a1 + _swap_halves(s) * a2 + v_ref[0, k]

    lax.fori_loop(0, nk, step, jnp.zeros((N_GROUPS, LANES), F32))


def _ssm_scan(v4, a1, a2):
    b, nk = v4.shape[0], v4.shape[1]
    blk = pl.BlockSpec((1, nk, N_GROUPS, LANES), lambda i: (i, 0, 0, 0))
    coef = pl.BlockSpec((N_GROUPS, LANES), lambda i: (0, 0))
    return pl.pallas_call(
        _ssm_scan_kernel,
        grid=(b,),
        in_specs=[blk, coef, coef],
        out_specs=blk,
        out_shape=jax.ShapeDtypeStruct(v4.shape, F32),
        compiler_params=_cparams(("parallel",)),
        name="ssm_scan",
    )(v4, a1, a2)


def _ssm_output_kernel(u_ref, s_ref, w2_ref, pt_ref, d_ref, z_ref, toep_ref):
    @pl.when(pl.program_id(0) == 0)
    def _():
        toep_ref[...] = jnp.zeros_like(toep_ref)

    nblk = CHUNK_W // LANES
    for i in range(nblk):
        toep_ref[i * LANES:(i + 1) * LANES, i * LANES:] = w2_ref[0, :, :CHUNK_W - i * LANES]

    u = u_ref[0]
    ub = u.astype(BF16)
    y = _dot_nt(s_ref[...].astype(BF16), pt_ref[0]) + d_ref[0] * u
    tile = 256
    cols = []
    for j in range(CHUNK_W // tile):
        kk = (j + 1) * tile
        cols.append(_dot(ub[:, :kk], toep_ref[:kk, j * tile:(j + 1) * tile]))
    y = y + jnp.concatenate(cols, axis=1)
    c0 = math.sqrt(2.0 / math.pi)
    z_ref[0] = (0.5 * y * (1.0 + jnp.tanh(c0 * (y + 0.044715 * (y * y * y))))).astype(BF16)


def _ssm_output(u, s, w2, pt, d_t):
    g, m, _ = u.shape
    return pl.pallas_call(
        _ssm_output_kernel,
        grid=(g,),
        in_specs=[pl.BlockSpec((1, m, CHUNK_W), lambda i: (i, 0, 0)),
                  pl.BlockSpec((m, LANES), lambda i: (0, i)),
                  pl.BlockSpec((1, LANES, CHUNK_W), lambda i: (i, 0, 0)),
                  pl.BlockSpec((1, CHUNK_W, LANES), lambda i: (i, 0, 0)),
                  pl.BlockSpec((1, 1, CHUNK_W), lambda i: (i, 0, 0))],
        out_specs=pl.BlockSpec((1, m, CHUNK_W), lambda i: (i, 0, 0)),
        out_shape=jax.ShapeDtypeStruct((g, m, CHUNK_W), BF16),
        scratch_shapes=[pltpu.VMEM((CHUNK_W, CHUNK_W), BF16)],
        compiler_params=_cparams(("arbitrary",)),
        name="ssm_output",
    )(u, s, w2, pt, d_t)


def _glu_kernel(h_ref, z_ref, w_ref, o_ref):
    zw = _dot(z_ref[...], w_ref[...])
    o_ref[...] = h_ref[...] + zw[:, :D_MODEL] * jax.nn.sigmoid(zw[:, D_MODEL:])


def _proj_res_kernel(h_ref, z_ref, w_ref, o_ref):
    o_ref[...] = h_ref[...] + _dot(z_ref[...], w_ref[...])


def _row_call(kernel, name, h2, z2, w):
    m = h2.shape[0]
    return pl.pallas_call(
        kernel,
        grid=(m // ROW_TILE,),
        in_specs=[pl.BlockSpec((ROW_TILE, D_MODEL), lambda i: (i, 0)),
                  pl.BlockSpec((ROW_TILE, z2.shape[1]), lambda i: (i, 0)),
                  pl.BlockSpec(w.shape, lambda i: (0, 0))],
        out_specs=pl.BlockSpec((ROW_TILE, D_MODEL), lambda i: (i, 0)),
        out_shape=jax.ShapeDtypeStruct((m, D_MODEL), F32),
        compiler_params=_cparams(("parallel",)),
        name=name,
    )(h2, z2, w)


def _mlp_kernel(h_ref, g_ref, w1_ref, w2_ref, fg_ref, o_ref, *, final_norm):
    x = h_ref[...]
    xn = _rms(x, g_ref[...]).astype(BF16)
    acc = x
    for f in range(0, D_FF, FF_CHUNK):
        a = jnp.square(jnp.maximum(_dot(xn, w1_ref[:, f:f + FF_CHUNK]), 0.0))
        acc = acc + _dot(a.astype(BF16), w2_ref[f:f + FF_CHUNK, :])
    if final_norm:
        acc = _rms(acc, fg_ref[...])
    o_ref[...] = acc


def _mlp(h2, g, w1, w2, fg, final_norm):
    m = h2.shape[0]
    return pl.pallas_call(
        functools.partial(_mlp_kernel, final_norm=final_norm),
        grid=(m // ROW_TILE,),
        in_specs=[pl.BlockSpec((ROW_TILE, D_MODEL), lambda i: (i, 0)),
                  pl.BlockSpec((1, D_MODEL), lambda i: (0, 0)),
                  pl.BlockSpec((D_MODEL, D_FF), lambda i: (0, 0)),
                  pl.BlockSpec((D_FF, D_MODEL), lambda i: (0, 0)),
                  pl.BlockSpec((1, D_MODEL), lambda i: (0, 0))],
        out_specs=pl.BlockSpec((ROW_TILE, D_MODEL), lambda i: (i, 0)),
        out_shape=jax.ShapeDtypeStruct((m, D_MODEL), F32),
        compiler_params=_cparams(("parallel",)),
        name="mlp_final" if final_norm else "mlp",
    )(h2, g.reshape(1, D_MODEL), w1, w2, fg.reshape(1, D_MODEL))


def _extra_scatter(sign_f, f_first):
    scat = np.zeros((N_EXTRA, LANES, N_HEADS * HEAD_PAD), np.float32)
    const = np.zeros((1, N_HEADS * HEAD_PAD), np.float32)
    for h in range(N_HEADS):
        base = h * HEAD_PAD + HEAD_DIM
        f0, o0 = (0, N_EXTRA) if f_first else (N_EXTRA, 0)
        for j in range(N_EXTRA):
            scat[j, h, base + f0 + j] = sign_f
            const[0, base + o0 + j] = 1.0
    return jnp.asarray(scat, BF16), jnp.asarray(const, F32)


def _kv_kernel(h_ref, g_ref, wk_ref, wv_ref, wf_ref, bf_ref, sc_ref, cst_ref,
               k_ref, v_ref, f_ref, carry_ref):
    j = pl.program_id(1)
    nt = pl.num_programs(1) - 1

    @pl.when(j == 0)
    def _():
        carry_ref[...] = jnp.zeros_like(carry_ref)

    @pl.when(j < nt)
    def _():
        xn = _rms(h_ref[0], g_ref[...]).astype(BF16)
        v_ref[0, 0] = _dot_nt(wv_ref[...], xn).astype(BF16)

        logit = _dot(xn, wf_ref[...]) + bf_ref[...]
        log_f = jnp.minimum(logit, 0.0) - jnp.log1p(jnp.exp(-jnp.abs(logit)))
        lane = lax.broadcasted_iota(jnp.int32, log_f.shape, 1)
        log_f = jnp.where(lane < N_HEADS, log_f, 0.0)
        t = log_f.shape[0]
        tri = (lax.broadcasted_iota(jnp.int32, (t, t), 0)
               >= lax.broadcasted_iota(jnp.int32, (t, t), 1)).astype(BF16)
        hi, mid, lo = _split3(log_f)
        cum = (_dot(tri, hi) + _dot(tri, mid)) + _dot(tri, lo) + carry_ref[...]
        carry_ref[...] = cum[t - 1:t, :]
        f_ref[0] = cum

        fh, fm, fl = _split3(cum * LOG2E)
        kaug = (_dot(xn, wk_ref[...]) + cst_ref[...]
                + _dot(fh, sc_ref[0]) + _dot(fm, sc_ref[1]) + _dot(fl, sc_ref[2])).astype(BF16)
        for hh in range(N_HEADS):
            k_ref[0, hh] = kaug[:, hh * HEAD_PAD:(hh + 1) * HEAD_PAD]

    @pl.when(j == nt)
    def _():
        lane = lax.broadcasted_iota(jnp.int32, (ROW_TILE, HEAD_PAD), 1)
        null_keys = jnp.where(lane == HEAD_DIM + N_EXTRA, NULL_KEY, 0.0).astype(BF16)
        for hh in range(N_HEADS):
            k_ref[0, hh] = null_keys


def _kv_proj(h3, g, wk_aug, wv, wf, bf):
    b, l, _ = h3.shape
    nt = l // ROW_TILE
    scat, const = _extra_scatter(-1.0, f_first=False)
    full = lambda a: pl.BlockSpec(a.shape, lambda i, j: (0,) * a.ndim)
    last = lambda j: jnp.minimum(j, nt - 1)
    return pl.pallas_call(
        _kv_kernel,
        grid=(b, nt + 1),
        in_specs=[pl.BlockSpec((1, ROW_TILE, D_MODEL), lambda i, j: (i, last(j), 0)),
                  pl.BlockSpec((1, D_MODEL), lambda i, j: (0, 0)),
                  full(wk_aug), full(wv), full(wf), full(bf), full(scat), full(const)],
        out_specs=[pl.BlockSpec((1, N_HEADS, ROW_TILE, HEAD_PAD), lambda i, j: (i, 0, j, 0)),
                   pl.BlockSpec((1, 1, D_MODEL, ROW_TILE), lambda i, j: (i, last(j), 0, 0)),
                   pl.BlockSpec((1, ROW_TILE, LANES), lambda i, j: (i, last(j), 0))],
        out_shape=[jax.ShapeDtypeStruct((b, N_HEADS, l + ROW_TILE, HEAD_PAD), BF16),
                   jax.ShapeDtypeStruct((b, nt, D_MODEL, ROW_TILE), BF16),
                   jax.ShapeDtypeStruct((b, l, LANES), F32)],
        scratch_shapes=[pltpu.VMEM((1, LANES), F32)],
        compiler_params=_cparams(("parallel", "arbitrary")),
        name="kv_proj",
    )(h3, g.reshape(1, D_MODEL), wk_aug, wv, wf, bf, scat, const)


def _q_kernel(h_ref, g_ref, wq_ref, f_ref, sc_ref, cst_ref, q_ref):
    xn = _rms(h_ref[0], g_ref[...]).astype(BF16)
    fh, fm, fl = _split3(f_ref[0] * LOG2E)
    qaug = (_dot(xn, wq_ref[...]) + cst_ref[...]
            + _dot(fh, sc_ref[0]) + _dot(fm, sc_ref[1]) + _dot(fl, sc_ref[2])).astype(BF16)
    for hh in range(N_HEADS):
        q_ref[0, hh] = qaug[:, hh * HEAD_PAD:(hh + 1) * HEAD_PAD]


def _q_proj(h3, g, wq_aug, fcum):
    b, l, _ = h3.shape
    scat, const = _extra_scatter(1.0, f_first=True)
    full = lambda a: pl.BlockSpec(a.shape, lambda i, j: (0,) * a.ndim)
    return pl.pallas_call(
        _q_kernel,
        grid=(b, l // ROW_TILE),
        in_specs=[pl.BlockSpec((1, ROW_TILE, D_MODEL), lambda i, j: (i, j, 0)),
                  pl.BlockSpec((1, D_MODEL), lambda i, j: (0, 0)),
                  full(wq_aug),
                  pl.BlockSpec((1, ROW_TILE, LANES), lambda i, j: (i, j, 0)),
                  full(scat), full(const)],
        out_specs=pl.BlockSpec((1, N_HEADS, ROW_TILE, HEAD_PAD), lambda i, j: (i, 0, j, 0)),
        out_shape=jax.ShapeDtypeStruct((b, N_HEADS, l, HEAD_PAD), BF16),
        compiler_params=_cparams(("parallel", "parallel")),
        name="q_proj",
    )(h3, g.reshape(1, D_MODEL), wq_aug, fcum, scat, const)


def _attn_kernel(q_ref, k_ref, vt_ref, o_ref, s_slot0, s_slot1, p_slot0, p_slot1):
    s_scr = (s_slot0, s_slot1)
    p_scr = (p_slot0, p_slot1)
    qi = pl.program_id(2)
    nb = (qi * ATT_BQ) // ATT_BK + 1
    lead = nb % 2
    null_blk = k_ref.shape[2] // ATT_BK - 1

    def step(t, cur, nxt, small):
        jk = jnp.where(t >= 0, t, null_blk)
        accs = []
        for hh in range(2):
            a_prev, _, _, acc, _ = small[hh]
            vt = vt_ref[0, jnp.maximum(t - 2, 0), pl.ds(hh * HEAD_DIM, HEAD_DIM), :]
            accs.append(a_prev * acc + _dot(vt, p_scr[cur][hh]))
        mb_next = []
        for hh in range(2):
            s = _dot_nt(k_ref[0, hh, pl.ds(jk * ATT_BK, ATT_BK), :], q_ref[0, hh])
            s_scr[nxt][hh] = s
            mb_next.append(jnp.max(s, axis=0, keepdims=True))
        out = []
        for hh in range(2):
            _, m, l, _, mb = small[hh]
            m_new = jnp.maximum(m, mb)
            p = jnp.exp2(s_scr[cur][hh] - m_new)
            alpha = jnp.exp2(m - m_new)
            l = alpha * l + jnp.sum(p, axis=0, keepdims=True)
            p_scr[nxt][hh] = p.astype(BF16)
            out.append((alpha, m_new, l, accs[hh], mb_next[hh]))
        return tuple(out)

    s_scr[0][...] = jnp.full(s_scr[0].shape, NULL_KEY, F32)
    p_scr[0][...] = jnp.zeros(p_scr[0].shape, BF16)
    small = tuple((jnp.ones((1, ATT_BQ), F32), jnp.full((1, ATT_BQ), NEG_BIG, F32),
                   jnp.zeros((1, ATT_BQ), F32), jnp.zeros((HEAD_DIM, ATT_BQ), F32),
                   jnp.full((1, ATT_BQ), NULL_KEY, F32))
                  for _ in range(2))

    def body(i, small):
        t = 2 * i - lead
        return step(t + 1, 1, 0, step(t, 0, 1, small))

    small = lax.fori_loop(0, (nb + lead) // 2, body, small)

    outs = []
    kpos = (nb - 1) * ATT_BK + lax.broadcasted_iota(jnp.int32, (ATT_BK, ATT_BQ), 0)
    qpos = qi * ATT_BQ + lax.broadcasted_iota(jnp.int32, (ATT_BK, ATT_BQ), 1)
    for hh in range(2):
        a_prev, m, l, acc, _ = small[hh]
        vt = vt_ref[0, jnp.maximum(nb - 2, 0), pl.ds(hh * HEAD_DIM, HEAD_DIM), :]
        acc = a_prev * acc + _dot(vt, p_scr[0][hh])
        s = jnp.where(qpos >= kpos, s_scr[0][hh], NEG_BIG)
        m_new = jnp.maximum(m, jnp.max(s, axis=0, keepdims=True))
        p = jnp.exp2(s - m_new)
        l = jnp.exp2(m - m_new) * l + jnp.sum(p, axis=0, keepdims=True)
        vt = vt_ref[0, nb - 1, pl.ds(hh * HEAD_DIM, HEAD_DIM), :]
        acc = jnp.exp2(m - m_new) * acc + _dot(vt, p.astype(BF16))
        outs.append(acc / l)
    o_ref[0] = jnp.concatenate(outs, axis=0).T.astype(BF16)


def _attention(q_aug, k_aug, vt):
    b, _, l, _ = q_aug.shape
    return pl.pallas_call(
        _attn_kernel,
        grid=(b, N_HEADS // 2, l // ATT_BQ),
        in_specs=[pl.BlockSpec((1, 2, ATT_BQ, HEAD_PAD), lambda i, h, q: (i, h, q, 0)),
                  pl.BlockSpec((1, 2, l + ATT_BK, HEAD_PAD), lambda i, h, q: (i, h, 0, 0)),
                  pl.BlockSpec((1, l // ATT_BK, 2 * HEAD_DIM, ATT_BK), lambda i, h, q: (i, 0, h, 0))],
        out_specs=pl.BlockSpec((1, ATT_BQ, 2 * HEAD_DIM), lambda i, h, q: (i, q, h)),
        out_shape=jax.ShapeDtypeStruct((b, l, D_MODEL), BF16),
        scratch_shapes=[pltpu.VMEM((2, ATT_BK, ATT_BQ), F32)] * 2
                       + [pltpu.VMEM((2, ATT_BK, ATT_BQ), BF16)] * 2,
        compiler_params=_cparams(("parallel", "parallel", "arbitrary")),
        name="fox_attention",
    )(q_aug, k_aug, vt)


def _pad_heads(w):
    w3 = w.reshape(D_MODEL, N_HEADS, HEAD_DIM)
    w3 = jnp.pad(w3, ((0, 0), (0, 0), (0, HEAD_PAD - HEAD_DIM)))
    return w3.reshape(D_MODEL, N_HEADS * HEAD_PAD)


def kernel(x, mix_norm, mlp_norm, mlp_w1, mlp_w2, ssm_log_dt, ssm_a_re, ssm_a_im,
           ssm_b_re, ssm_b_im, ssm_c_re, ssm_c_im, ssm_d, ssm_w_glu, kv_norm, w_kvf, b_f,
           attn_wq, attn_wo, final_norm):
    bsz, length, _ = x.shape
    m = bsz * length
    nk = length // SSM_CHUNK
    h = x.reshape(m, D_MODEL)

    for i in range(DEPTH):
        if i < N_A_LAYERS:
            w2, pt, r, a1, a2 = _ssm_prep(ssm_log_dt[i], ssm_a_re[i], ssm_a_im[i], ssm_b_re[i],
                                          ssm_b_im[i], ssm_c_re[i], ssm_c_im[i])
            hn = _rmsnorm(h, mix_norm[i])
            u = hn.reshape(bsz * nk, SSM_CHUNK, N_GROUPS, SSM_GROUP).transpose(2, 0, 1, 3)
            u = u.reshape(N_GROUPS, bsz * nk, CHUNK_W)
            v = _ssm_summary(u, r)
            s = _ssm_scan(v.reshape(bsz, nk, N_GROUPS, LANES),
                          a1.reshape(N_GROUPS, LANES), a2.reshape(N_GROUPS, LANES))
            d_t = jnp.tile(ssm_d[i].reshape(N_GROUPS, 1, SSM_GROUP), (1, 1, SSM_CHUNK))
            z = _ssm_output(u, s.reshape(bsz * nk, N_GROUPS * LANES), w2, pt, d_t)
            z = z.reshape(N_GROUPS, bsz * nk, SSM_CHUNK, SSM_GROUP).transpose(1, 2, 0, 3)
            h = _row_call(_glu_kernel, "glu", h, z.reshape(m, D_MODEL), ssm_w_glu[i].astype(BF16))
        else:
            j = i - N_A_LAYERS
            wq_aug = _pad_heads(attn_wq[j] * (HEAD_DIM ** -0.5 * LOG2E)).astype(BF16)
            q_aug = _q_proj(h.reshape(bsz, length, D_MODEL), mix_norm[i], wq_aug, fcum)
            o = _attention(q_aug, k_aug, v_nat)
            h = _row_call(_proj_res_kernel, "attn_out", h, o.reshape(m, D_MODEL),
                          attn_wo[j].astype(BF16))
        h = _mlp(h, mlp_norm[i], mlp_w1[i].astype(BF16), mlp_w2[i].astype(BF16),
                 final_norm, final_norm=(i == DEPTH - 1))
        if i == N_A_LAYERS - 1:
            attn_dim = N_HEADS * HEAD_DIM
            wk_aug = _pad_heads(w_kvf[:, :attn_dim]).astype(BF16)
            wv = w_kvf[:, attn_dim:2 * attn_dim].T.astype(BF16)
            wf = jnp.pad(w_kvf[:, 2 * attn_dim:], ((0, 0), (0, LANES - N_HEADS))).astype(BF16)
            bf = jnp.pad(b_f, (0, LANES - N_HEADS)).reshape(1, LANES)
            k_aug, v_nat, fcum = _kv_proj(h.reshape(bsz, length, D_MODEL), kv_norm,
                                          wk_aug, wv, wf, bf)
    return h.reshape(bsz, length, D_MODEL)
```

```python
import functools
import math

import numpy as np
import jax
import jax.numpy as jnp
from jax import lax
from jax.experimental import pallas as pl
from jax.experimental.pallas import tpu as pltpu

D_MODEL = 1024
N_GROUPS = 64
SSM_GROUP = 16
GROUP_SHIFT = 4
SSM_STATE = 64
N_HEADS = 16
HEAD_DIM = 64
D_FF = 4 * D_MODEL
RMS_EPS = 1e-6
N_A_LAYERS = 2
DEPTH = 4

SSM_CHUNK = 64
CHUNK_W = SSM_CHUNK * SSM_GROUP
LANES = 128
HEAD_PAD = 128
N_EXTRA = 3

ROW_TILE = 512
FF_CHUNK = 1024
ATT_BQ = 256
VT_ROWS = 80
ATT_HEADS = 2
ATT_BK = 512
NEG_BIG = -1e30
NULL_KEY = 3.0 * NEG_BIG
LOG2E = math.log2(math.e)

assert ROW_TILE == ATT_BK and ATT_BK % ATT_BQ == 0

VMEM_LIMIT = 56 * 1024 * 1024

F32 = jnp.float32
BF16 = jnp.bfloat16


def _cparams(sem):
    return pltpu.CompilerParams(dimension_semantics=sem, vmem_limit_bytes=VMEM_LIMIT)


def _rms(x, g):
    return x * lax.rsqrt(jnp.mean(x * x, axis=-1, keepdims=True) + RMS_EPS) * g


def _dot(a, b):
    return jnp.dot(a, b, preferred_element_type=F32)


def _dot_nt(a, b, precision=None):
    return lax.dot_general(a, b, (((1,), (1,)), ((), ())),
                           preferred_element_type=F32, precision=precision)


def _split3(x):
    hi = x.astype(BF16)
    r1 = x - hi.astype(F32)
    mid = r1.astype(BF16)
    lo = (r1 - mid.astype(F32)).astype(BF16)
    return hi, mid, lo


def _rmsnorm_kernel(h_ref, g_ref, o_ref):
    o_ref[...] = _rms(h_ref[...], g_ref[...])


def _rmsnorm(h2, g):
    m = h2.shape[0]
    return pl.pallas_call(
        _rmsnorm_kernel,
        grid=(m // ROW_TILE,),
        in_specs=[pl.BlockSpec((ROW_TILE, D_MODEL), lambda i: (i, 0)),
                  pl.BlockSpec((1, D_MODEL), lambda i: (0, 0))],
        out_specs=pl.BlockSpec((ROW_TILE, D_MODEL), lambda i: (i, 0)),
        out_shape=jax.ShapeDtypeStruct((m, D_MODEL), F32),
        compiler_params=_cparams(("parallel",)),
        name="rmsnorm",
    )(h2, g.reshape(1, D_MODEL))


def _swap_halves(x):
    return pltpu.roll(x, 64, axis=x.ndim - 1)


def _ssm_prep_kernel(ldt_ref, are_ref, aim_ref, bt_ref, cc_ref,
                     w2_ref, pt_ref, r_ref, a1_ref, a2_ref):
    lane1 = lax.broadcasted_iota(jnp.int32, (1, LANES), 1)
    lo1 = lane1 < SSM_STATE
    a_re = are_ref[0]
    a_im = aim_ref[0]
    dt = jnp.exp(ldt_ref[0])
    zr = a_re * dt
    zi = a_im * dt
    mag = jnp.exp(zr)
    lr = mag * jnp.cos(zi)
    li = mag * jnp.sin(zi)
    imag = jnp.exp(-zr)
    ir = imag * jnp.cos(zi)
    ii = -imag * jnp.sin(zi)

    def cmul_s(x, wr, wi):
        return x * wr + _swap_halves(x) * jnp.where(lo1, -wi, wi)

    def cmul_g(x, y):
        ys = _swap_halves(y)
        lo = lax.broadcasted_iota(jnp.int32, y.shape, 1) < SSM_STATE
        return x * jnp.where(lo, y, ys) + _swap_halves(x) * jnp.where(lo, -ys, y)

    def powers(n, nbits, wr, wi):
        p = jnp.broadcast_to(jnp.where(lo1, 1.0, 0.0).astype(F32), (n.shape[0], LANES))
        for k in range(nbits):
            bit = ((n >> k) & 1) == 1
            p = jnp.where(bit, cmul_s(p, wr, wi), p)
            wr, wi = wr * wr - wi * wi, 2.0 * wr * wi
        return p

    nr = lr - 1.0
    den = a_re * a_re + a_im * a_im
    cr = (nr * a_re + li * a_im) / den
    ci = (li * a_re - nr * a_im) / den
    bbar = cmul_s(bt_ref[0], cr, ci)
    cc = cc_ref[0]

    def tile_rows(x, reps):
        return jnp.concatenate([x] * reps, axis=0)

    row = lax.broadcasted_iota(jnp.int32, (CHUNK_W, 1), 0)
    n_big = row >> GROUP_SHIFT
    e_pos = powers(n_big, 6, lr, li)
    cc_t = tile_rows(cc, SSM_CHUNK)
    bb_t = tile_rows(bbar, SSM_CHUNK)
    lo_big = lax.broadcasted_iota(jnp.int32, (CHUNK_W, LANES), 1) < SSM_STATE

    cbig = cmul_g(e_pos, cc_t)
    cl1 = cmul_s(cbig, lr, li)
    pt_ref[0] = jnp.where(lo_big, cl1, -cl1).astype(BF16)
    e_rev = powers(SSM_CHUNK - 1 - n_big, 6, lr, li)
    r_ref[0] = cmul_g(e_rev, bb_t).astype(BF16)

    row8 = lax.broadcasted_iota(jnp.int32, (LANES, 1), 0)
    n8 = row8 >> GROUP_SHIFT
    bsmall = cmul_g(powers(n8, 3, ir, ii), tile_rows(bbar, LANES // SSM_GROUP))
    lo8 = lax.broadcasted_iota(jnp.int32, (LANES, LANES), 1) < SSM_STATE
    lhs = jnp.where(lo8, bsmall, -bsmall)
    w2 = _dot_nt(lhs, cbig, precision=lax.Precision.HIGHEST)
    col_t = lax.broadcasted_iota(jnp.int32, (LANES, CHUNK_W), 1) >> GROUP_SHIFT
    w2_ref[0] = jnp.where(col_t >= n8, w2, 0.0).astype(BF16)

    ar, ai = lr, li
    for _ in range(6):
        ar, ai = ar * ar - ai * ai, 2.0 * ar * ai
    a1_ref[0] = ar
    a2_ref[0] = jnp.where(lo1, -ai, ai)


def _ssm_prep(log_dt, a_re, a_im, b_re, b_im, c_re, c_im):
    g = N_GROUPS
    dup = lambda x: jnp.concatenate([x, x], axis=-1).reshape(g, 1, LANES)
    ldt = jnp.broadcast_to(log_dt.reshape(g, 1, 1), (g, 1, LANES))
    bt = jnp.concatenate([b_re.transpose(0, 2, 1), b_im.transpose(0, 2, 1)], axis=-1)
    cc = jnp.concatenate([c_re, c_im], axis=-1)
    vec = pl.BlockSpec((1, 1, LANES), lambda i: (i, 0, 0))
    mat = pl.BlockSpec((1, SSM_GROUP, LANES), lambda i: (i, 0, 0))
    return pl.pallas_call(
        _ssm_prep_kernel,
        grid=(g,),
        in_specs=[vec, vec, vec, mat, mat],
        out_specs=[pl.BlockSpec((1, LANES, CHUNK_W), lambda i: (i, 0, 0)),
                   pl.BlockSpec((1, CHUNK_W, LANES), lambda i: (i, 0, 0)),
                   pl.BlockSpec((1, CHUNK_W, LANES), lambda i: (i, 0, 0)),
                   vec, vec],
        out_shape=[jax.ShapeDtypeStruct((g, LANES, CHUNK_W), BF16),
                   jax.ShapeDtypeStruct((g, CHUNK_W, LANES), BF16),
                   jax.ShapeDtypeStruct((g, CHUNK_W, LANES), BF16),
                   jax.ShapeDtypeStruct((g, 1, LANES), F32),
                   jax.ShapeDtypeStruct((g, 1, LANES), F32)],
        compiler_params=_cparams(("parallel",)),
        name="ssm_prep",
    )(ldt, dup(a_re), dup(a_im), bt, cc)


def _ssm_summary_kernel(u_ref, r_ref, v_ref):
    v_ref[...] = _dot(u_ref[0].astype(BF16), r_ref[0])


def _ssm_summary(u, r):
    g, m, _ = u.shape
    return pl.pallas_call(
        _ssm_summary_kernel,
        grid=(g,),
        in_specs=[pl.BlockSpec((1, m, CHUNK_W), lambda i: (i, 0, 0)),
                  pl.BlockSpec((1, CHUNK_W, LANES), lambda i: (i, 0, 0))],
        out_specs=pl.BlockSpec((m, LANES), lambda i: (0, i)),
        out_shape=jax.ShapeDtypeStruct((m, g * LANES), F32),
        compiler_params=_cparams(("parallel",)),
        name="ssm_summary",
    )(u, r)


def _ssm_scan_kernel(v_ref, a1_ref, a2_ref, s_ref):
    a1 = a1_ref[...]
    a2 = a2_ref[...]
    nk = v_ref.shape[1]

    def step(k, s):
        s_ref[0, k] = s
        return s * a1 + _swap_halves(s) * a2 + v_ref[0, k]

    lax.fori_loop(0, nk, step, jnp.zeros((N_GROUPS, LANES), F32))


def _ssm_scan(v4, a1, a2):
    b, nk = v4.shape[0], v4.shape[1]
    blk = pl.BlockSpec((1, nk, N_GROUPS, LANES), lambda i: (i, 0, 0, 0))
    coef = pl.BlockSpec((N_GROUPS, LANES), lambda i: (0, 0))
    return pl.pallas_call(
        _ssm_scan_kernel,
        grid=(b,),
        in_specs=[blk, coef, coef],
        out_specs=blk,
        out_shape=jax.ShapeDtypeStruct(v4.shape, F32),
        compiler_params=_cparams(("parallel",)),
        name="ssm_scan",
    )(v4, a1, a2)


def _ssm_output_kernel(u_ref, s_ref, w2_ref, pt_ref, d_ref, z_ref, toep_ref):
    @pl.when(pl.program_id(0) == 0)
    def _():
        toep_ref[...] = jnp.zeros_like(toep_ref)

    nblk = CHUNK_W // LANES
    for i in range(nblk):
        toep_ref[i * LANES:(i + 1) * LANES, i * LANES:] = w2_ref[0, :, :CHUNK_W - i * LANES]

    u = u_ref[0]
    ub = u.astype(BF16)
    y = _dot_nt(s_ref[...].astype(BF16), pt_ref[0]) + d_ref[0] * u
    tile = 256
    cols = []
    for j in range(CHUNK_W // tile):
        kk = (j + 1) * tile
        cols.append(_dot(ub[:, :kk], toep_ref[:kk, j * tile:(j + 1) * tile]))
    y = y + jnp.concatenate(cols, axis=1)
    c0 = math.sqrt(2.0 / math.pi)
    z_ref[0] = (0.5 * y * (1.0 + jnp.tanh(c0 * (y + 0.044715 * (y * y * y))))).astype(BF16)


def _ssm_output(u, s, w2, pt, d_t):
    g, m, _ = u.shape
    return pl.pallas_call(
        _ssm_output_kernel,
        grid=(g,),
        in_specs=[pl.BlockSpec((1, m, CHUNK_W), lambda i: (i, 0, 0)),
                  pl.BlockSpec((m, LANES), lambda i: (0, i)),
                  pl.BlockSpec((1, LANES, CHUNK_W), lambda i: (i, 0, 0)),
                  pl.BlockSpec((1, CHUNK_W, LANES), lambda i: (i, 0, 0)),
                  pl.BlockSpec((1, 1, CHUNK_W), lambda i: (i, 0, 0))],
        out_specs=pl.BlockSpec((1, m, CHUNK_W), lambda i: (i, 0, 0)),
        out_shape=jax.ShapeDtypeStruct((g, m, CHUNK_W), BF16),
        scratch_shapes=[pltpu.VMEM((CHUNK_W, CHUNK_W), BF16)],
        compiler_params=_cparams(("arbitrary",)),
        name="ssm_output",
    )(u, s, w2, pt, d_t)


def _glu_kernel(h_ref, z_ref, w_ref, o_ref):
    zw = _dot(z_ref[...], w_ref[...])
    o_ref[...] = h_ref[...] + zw[:, :D_MODEL] * jax.nn.sigmoid(zw[:, D_MODEL:])


def _proj_res_kernel(h_ref, z_ref, w_ref, o_ref):
    o_ref[...] = h_ref[...] + _dot(z_ref[...], w_ref[...])


def _row_call(kernel, name, h2, z2, w):
    m = h2.shape[0]
    return pl.pallas_call(
        kernel,
        grid=(m // ROW_TILE,),
        in_specs=[pl.BlockSpec((ROW_TILE, D_MODEL), lambda i: (i, 0)),
                  pl.BlockSpec((ROW_TILE, z2.shape[1]), lambda i: (i, 0)),
                  pl.BlockSpec(w.shape, lambda i: (0, 0))],
        out_specs=pl.BlockSpec((ROW_TILE, D_MODEL), lambda i: (i, 0)),
        out_shape=jax.ShapeDtypeStruct((m, D_MODEL), F32),
        compiler_params=_cparams(("parallel",)),
        name=name,
    )(h2, z2, w)


def _mlp_kernel(h_ref, g_ref, w1_ref, w2_ref, fg_ref, o_ref, *, final_norm):
    x = h_ref[...]
    xn = _rms(x, g_ref[...]).astype(BF16)
    acc = x
    for f in range(0, D_FF, FF_CHUNK):
        a = jnp.square(jnp.maximum(_dot(xn, w1_ref[:, f:f + FF_CHUNK]), 0.0))
        acc = acc + _dot(a.astype(BF16), w2_ref[f:f + FF_CHUNK, :])
    if final_norm:
        acc = _rms(acc, fg_ref[...])
    o_ref[...] = acc


def _mlp(h2, g, w1, w2, fg, final_norm):
    m = h2.shape[0]
    return pl.pallas_call(
        functools.partial(_mlp_kernel, final_norm=final_norm),
        grid=(m // ROW_TILE,),
        in_specs=[pl.BlockSpec((ROW_TILE, D_MODEL), lambda i: (i, 0)),
                  pl.BlockSpec((1, D_MODEL), lambda i: (0, 0)),
                  pl.BlockSpec((D_MODEL, D_FF), lambda i: (0, 0)),
                  pl.BlockSpec((D_FF, D_MODEL), lambda i: (0, 0)),
                  pl.BlockSpec((1, D_MODEL), lambda i: (0, 0))],
        out_specs=pl.BlockSpec((ROW_TILE, D_MODEL), lambda i: (i, 0)),
        out_shape=jax.ShapeDtypeStruct((m, D_MODEL), F32),
        compiler_params=_cparams(("parallel",)),
        name="mlp_final" if final_norm else "mlp",
    )(h2, g.reshape(1, D_MODEL), w1, w2, fg.reshape(1, D_MODEL))


def _extra_scatter(sign_f, f_first):
    scat = np.zeros((N_EXTRA, LANES, N_HEADS * HEAD_PAD), np.float32)
    const = np.zeros((1, N_HEADS * HEAD_PAD), np.float32)
    for h in range(N_HEADS):
        base = h * HEAD_PAD + HEAD_DIM
        f0, o0 = (0, N_EXTRA) if f_first else (N_EXTRA, 0)
        for j in range(N_EXTRA):
            scat[j, h, base + f0 + j] = sign_f
            const[0, base + o0 + j] = 1.0
    return jnp.asarray(scat, BF16), jnp.asarray(const, F32)


def _kv_kernel(h_ref, g_ref, wk_ref, wv_ref, vone_ref, wf_ref, bf_ref, sc_ref, cst_ref,
               k_ref, v_ref, f_ref, carry_ref):
    @pl.when(pl.program_id(1) == 0)
    def _():
        carry_ref[...] = jnp.zeros_like(carry_ref)

    xn = _rms(h_ref[0], g_ref[...]).astype(BF16)
    v_ref[0, 0] = (_dot_nt(wv_ref[...], xn) + vone_ref[...]).astype(BF16)

    logit = _dot(xn, wf_ref[...]) + bf_ref[...]
    log_f = jnp.minimum(logit, 0.0) - jnp.log1p(jnp.exp(-jnp.abs(logit)))
    lane = lax.broadcasted_iota(jnp.int32, log_f.shape, 1)
    log_f = jnp.where(lane < N_HEADS, log_f, 0.0)
    t = log_f.shape[0]
    tri = (lax.broadcasted_iota(jnp.int32, (t, t), 0)
           >= lax.broadcasted_iota(jnp.int32, (t, t), 1)).astype(BF16)
    hi, mid, lo = _split3(log_f)
    cum = (_dot(tri, hi) + _dot(tri, mid)) + _dot(tri, lo) + carry_ref[...]
    carry_ref[...] = cum[t - 1:t, :]
    f_ref[0] = cum

    fh, fm, fl = _split3(cum * LOG2E)
    kaug = (_dot(xn, wk_ref[...]) + cst_ref[...]
            + _dot(fh, sc_ref[0]) + _dot(fm, sc_ref[1]) + _dot(fl, sc_ref[2])).astype(BF16)
    for hh in range(N_HEADS):
        k_ref[0, hh] = kaug[:, hh * HEAD_PAD:(hh + 1) * HEAD_PAD]


def _kv_proj(h3, g, wk_aug, wv, wf, bf):
    b, l, _ = h3.shape
    nt = l // ROW_TILE
    scat, const = _extra_scatter(-1.0, f_first=False)
    vone = np.zeros((N_HEADS, VT_ROWS, 1), np.float32)
    vone[:, HEAD_DIM] = 1.0
    vone = jnp.asarray(vone.reshape(N_HEADS * VT_ROWS, 1))
    full = lambda a: pl.BlockSpec(a.shape, lambda i, j: (0,) * a.ndim)
    return pl.pallas_call(
        _kv_kernel,
        grid=(b, nt),
        in_specs=[pl.BlockSpec((1, ROW_TILE, D_MODEL), lambda i, j: (i, j, 0)),
                  pl.BlockSpec((1, D_MODEL), lambda i, j: (0, 0)),
                  full(wk_aug), full(wv), full(vone), full(wf), full(bf), full(scat), full(const)],
        out_specs=[pl.BlockSpec((1, N_HEADS, ROW_TILE, HEAD_PAD), lambda i, j: (i, 0, j, 0)),
                   pl.BlockSpec((1, 1, N_HEADS * VT_ROWS, ROW_TILE), lambda i, j: (i, j, 0, 0)),
                   pl.BlockSpec((1, ROW_TILE, LANES), lambda i, j: (i, j, 0))],
        out_shape=[jax.ShapeDtypeStruct((b, N_HEADS, l, HEAD_PAD), BF16),
                   jax.ShapeDtypeStruct((b, nt, N_HEADS * VT_ROWS, ROW_TILE), BF16),
                   jax.ShapeDtypeStruct((b, l, LANES), F32)],
        scratch_shapes=[pltpu.VMEM((1, LANES), F32)],
        compiler_params=_cparams(("parallel", "arbitrary")),
        name="kv_proj",
    )(h3, g.reshape(1, D_MODEL), wk_aug, wv, vone, wf, bf, scat, const)


def _q_kernel(h_ref, g_ref, wq_ref, f_ref, sc_ref, cst_ref, q_ref):
    xn = _rms(h_ref[0], g_ref[...]).astype(BF16)
    fh, fm, fl = _split3(f_ref[0] * LOG2E)
    qaug = (_dot(xn, wq_ref[...]) + cst_ref[...]
            + _dot(fh, sc_ref[0]) + _dot(fm, sc_ref[1]) + _dot(fl, sc_ref[2])).astype(BF16)
    for hh in range(N_HEADS):
        q_ref[0, hh] = qaug[:, hh * HEAD_PAD:(hh + 1) * HEAD_PAD]


def _q_proj(h3, g, wq_aug, fcum):
    b, l, _ = h3.shape
    scat, const = _extra_scatter(1.0, f_first=True)
    full = lambda a: pl.BlockSpec(a.shape, lambda i, j: (0,) * a.ndim)
    return pl.pallas_call(
        _q_kernel,
        grid=(b, l // ROW_TILE),
        in_specs=[pl.BlockSpec((1, ROW_TILE, D_MODEL), lambda i, j: (i, j, 0)),
                  pl.BlockSpec((1, D_MODEL), lambda i, j: (0, 0)),
                  full(wq_aug),
                  pl.BlockSpec((1, ROW_TILE, LANES), lambda i, j: (i, j, 0)),
                  full(scat), full(const)],
        out_specs=pl.BlockSpec((1, N_HEADS, ROW_TILE, HEAD_PAD), lambda i, j: (i, 0, j, 0)),
        out_shape=jax.ShapeDtypeStruct((b, N_HEADS, l, HEAD_PAD), BF16),
        compiler_params=_cparams(("parallel", "parallel")),
        name="q_proj",
    )(h3, g.reshape(1, D_MODEL), wq_aug, fcum, scat, const)


def _causal_items(length):
    return sum((qi * ATT_BQ) // ATT_BK + 1 for qi in range(length // ATT_BQ))


def _diag_bias():
    k = np.arange(ATT_BK)[:, None]
    q = np.arange(ATT_BQ)[None, :]
    tabs = [np.zeros((ATT_BK, ATT_BQ), np.float32)]
    for r in range(ATT_BK // ATT_BQ):
        tabs.append(np.where(k - q <= r * ATT_BQ, 0.0, NEG_BIG).astype(np.float32))
    return jnp.asarray(np.stack(tabs))


def _attn_kernel(q_ref, k_ref, vt_ref, bias_ref, o_ref, s_slot0, s_slot1, p_slot0, p_slot1, acc_scr):
    s_scr = (s_slot0, s_slot1)
    p_scr = (p_slot0, p_slot1)
    length = q_ref.shape[2]
    n_q = length // ATT_BQ
    n_items = _causal_items(length)
    assert n_items % 2 == 0

    def blocks_of(qi):
        return (qi * ATT_BQ) // ATT_BK + 1

    def advance(item):
        qi, j = item
        last = j + 1 == blocks_of(qi)
        return (jnp.minimum(jnp.where(last, qi + 1, qi), n_q - 1), jnp.where(last, 0, j + 1))

    def step(slot, item_s, item_x, item_v, chain, lane):
        m, acc = chain
        mb, alpha_q = lane
        out_chain, out_lane = [], []
        qi_v, j_v = item_v
        for hh in range(ATT_HEADS):
            vt = vt_ref[0, j_v, pl.ds(hh * VT_ROWS, VT_ROWS), :]
            acc_h = alpha_q[hh] * acc[hh] + _dot(vt, p_scr[slot][hh])
            acc_scr[qi_v, hh] = acc_h
            out_chain.append([None, acc_h])
        _, j_x = item_x
        for hh in range(ATT_HEADS):
            m_in = jnp.where(j_x == 0, NEG_BIG, m[hh])
            m_new = jnp.maximum(m_in, mb[hh])
            p_scr[slot][hh] = jnp.exp2(s_scr[slot][hh] - m_new).astype(BF16)
            out_chain[hh][0] = m_new
            out_lane.append([None, jnp.exp2(m_in - m_new)])
        qi_s, j_s = item_s
        shift = qi_s * ATT_BQ - j_s * ATT_BK
        sel = jnp.where(j_s + 1 == blocks_of(qi_s), 1 + shift // ATT_BQ, 0)
        for hh in range(ATT_HEADS):
            s = _dot_nt(k_ref[0, hh, pl.ds(j_s * ATT_BK, ATT_BK), :],
                        q_ref[0, hh, pl.ds(qi_s * ATT_BQ, ATT_BQ), :]) + bias_ref[sel]
            s_scr[slot][hh] = s
            out_lane[hh][0] = jnp.max(s, axis=0, keepdims=True)
        chain = tuple(tuple(c[i] for c in out_chain) for i in range(2))
        lane = tuple(tuple(c[i] for c in out_lane) for i in range(2))
        return chain, lane

    for slot in range(2):
        s_scr[slot][...] = jnp.full(s_scr[slot].shape, NULL_KEY, F32)
        p_scr[slot][...] = jnp.zeros(p_scr[slot].shape, BF16)
    def per_head(shape, v):
        pos = lax.broadcasted_iota(jnp.int32, shape, 0) + lax.broadcasted_iota(jnp.int32, shape, 1)
        return tuple(jnp.where(pos >= 0, v, 0.0).astype(F32) for _ in range(ATT_HEADS))

    row = (1, ATT_BQ)
    chain = (per_head(row, NEG_BIG), per_head((VT_ROWS, ATT_BQ), 0.0))
    lane = (per_head(row, NULL_KEY), per_head(row, 1.0))
    zero = jnp.int32(0)
    first = (zero, zero)

    def body(_, carry):
        item, hist0, hist1, chain, lane0, lane1 = carry
        item1 = advance(item)
        chain, lane0 = step(0, item, hist0[0], hist0[1], chain, lane0)
        chain, lane1 = step(1, item1, hist1[0], hist1[1], chain, lane1)
        return (advance(item1), (item, hist0[0]), (item1, hist1[0]), chain, lane0, lane1)

    lax.fori_loop(0, n_items // 2 + 2, body,
                  (first, (first, first), (first, first), chain, lane, lane))

    def finish(qi, _):
        outs = []
        for hh in range(ATT_HEADS):
            a = acc_scr[qi, hh]
            outs.append(a[:HEAD_DIM] / a[HEAD_DIM:HEAD_DIM + 1])
        o_ref[0, pl.ds(qi * ATT_BQ, ATT_BQ), :] = jnp.concatenate(outs, axis=0).T.astype(BF16)
        return 0

    lax.fori_loop(0, n_q, finish, 0)


def _attention(q_aug, k_aug, vt):
    b, _, l, _ = q_aug.shape
    bias = _diag_bias()
    return pl.pallas_call(
        _attn_kernel,
        grid=(b, N_HEADS // ATT_HEADS),
        in_specs=[pl.BlockSpec((1, ATT_HEADS, l, HEAD_PAD), lambda i, h: (i, h, 0, 0)),
                  pl.BlockSpec((1, ATT_HEADS, l, HEAD_PAD), lambda i, h: (i, h, 0, 0)),
                  pl.BlockSpec((1, l // ATT_BK, ATT_HEADS * VT_ROWS, ATT_BK), lambda i, h: (i, 0, h, 0)),
                  pl.BlockSpec(bias.shape, lambda i, h: (0, 0, 0))],
        out_specs=pl.BlockSpec((1, l, ATT_HEADS * HEAD_DIM), lambda i, h: (i, 0, h)),
        out_shape=jax.ShapeDtypeStruct((b, l, D_MODEL), BF16),
        scratch_shapes=[pltpu.VMEM((ATT_HEADS, ATT_BK, ATT_BQ), F32)] * 2
                       + [pltpu.VMEM((ATT_HEADS, ATT_BK, ATT_BQ), BF16)] * 2
                       + [pltpu.VMEM((l // ATT_BQ, ATT_HEADS, VT_ROWS, ATT_BQ), F32)],
        compiler_params=_cparams(("parallel", "parallel")),
        name="fox_attention",
    )(q_aug, k_aug, vt, bias)


def _pad_heads(w):
    w3 = w.reshape(D_MODEL, N_HEADS, HEAD_DIM)
    w3 = jnp.pad(w3, ((0, 0), (0, 0), (0, HEAD_PAD - HEAD_DIM)))
    return w3.reshape(D_MODEL, N_HEADS * HEAD_PAD)


def kernel(x, mix_norm, mlp_norm, mlp_w1, mlp_w2, ssm_log_dt, ssm_a_re, ssm_a_im,
           ssm_b_re, ssm_b_im, ssm_c_re, ssm_c_im, ssm_d, ssm_w_glu, kv_norm, w_kvf, b_f,
           attn_wq, attn_wo, final_norm):
    bsz, length, _ = x.shape
    m = bsz * length
    nk = length // SSM_CHUNK
    h = x.reshape(m, D_MODEL)

    for i in range(DEPTH):
        if i < N_A_LAYERS:
            w2, pt, r, a1, a2 = _ssm_prep(ssm_log_dt[i], ssm_a_re[i], ssm_a_im[i], ssm_b_re[i],
                                          ssm_b_im[i], ssm_c_re[i], ssm_c_im[i])
            hn = _rmsnorm(h, mix_norm[i])
            u = hn.reshape(bsz * nk, SSM_CHUNK, N_GROUPS, SSM_GROUP).transpose(2, 0, 1, 3)
            u = u.reshape(N_GROUPS, bsz * nk, CHUNK_W)
            v = _ssm_summary(u, r)
            s = _ssm_scan(v.reshape(bsz, nk, N_GROUPS, LANES),
                          a1.reshape(N_GROUPS, LANES), a2.reshape(N_GROUPS, LANES))
            d_t = jnp.tile(ssm_d[i].reshape(N_GROUPS, 1, SSM_GROUP), (1, 1, SSM_CHUNK))
            z = _ssm_output(u, s.reshape(bsz * nk, N_GROUPS * LANES), w2, pt, d_t)
            z = z.reshape(N_GROUPS, bsz * nk, SSM_CHUNK, SSM_GROUP).transpose(1, 2, 0, 3)
            h = _row_call(_glu_kernel, "glu", h, z.reshape(m, D_MODEL), ssm_w_glu[i].astype(BF16))
        else:
            j = i - N_A_LAYERS
            wq_aug = _pad_heads(attn_wq[j] * (HEAD_DIM ** -0.5 * LOG2E)).astype(BF16)
            q_aug = _q_proj(h.reshape(bsz, length, D_MODEL), mix_norm[i], wq_aug, fcum)
            o = _attention(q_aug, k_aug, v_nat)
            h = _row_call(_proj_res_kernel, "attn_out", h, o.reshape(m, D_MODEL),
                          attn_wo[j].astype(BF16))
        h = _mlp(h, mlp_norm[i], mlp_w1[i].astype(BF16), mlp_w2[i].astype(BF16),
                 final_norm, final_norm=(i == DEPTH - 1))
        if i == N_A_LAYERS - 1:
            attn_dim = N_HEADS * HEAD_DIM
            wk_aug = _pad_heads(w_kvf[:, :attn_dim]).astype(BF16)
            wv = w_kvf[:, attn_dim:2 * attn_dim].T.reshape(N_HEADS, HEAD_DIM, D_MODEL)
            wv = jnp.pad(wv, ((0, 0), (0, VT_ROWS - HEAD_DIM), (0, 0)))
            wv = wv.reshape(N_HEADS * VT_ROWS, D_MODEL).astype(BF16)
            wf = jnp.pad(w_kvf[:, 2 * attn_dim:], ((0, 0), (0, LANES - N_HEADS))).astype(BF16)
            bf = jnp.pad(b_f, (0, LANES - N_HEADS)).reshape(1, LANES)
            k_aug, v_nat, fcum = _kv_proj(h.reshape(bsz, length, D_MODEL), kv_norm,
                                          wk_aug, wv, wf, bf)
    return h.reshape(bsz, length, D_MODEL)
```

```python
import functools
import math

import numpy as np
import jax
import jax.numpy as jnp
from jax import lax
from jax.experimental import pallas as pl
from jax.experimental.pallas import tpu as pltpu

D_MODEL = 1024
N_GROUPS = 64
SSM_GROUP = 16
GROUP_SHIFT = 4
SSM_STATE = 64
N_HEADS = 16
HEAD_DIM = 64
D_FF = 4 * D_MODEL
RMS_EPS = 1e-6
N_A_LAYERS = 2
DEPTH = 4

SSM_CHUNK = 64
CHUNK_W = SSM_CHUNK * SSM_GROUP
LANES = 128
HEAD_PAD = 128
N_EXTRA = 3

ROW_TILE = 512
FF_CHUNK = 1024
ATT_BQ = 256
VT_ROWS = 80
ATT_SLOTS = 8
ATT_HEADS = 2
ATT_BK = 512
NEG_BIG = -1e30
NULL_KEY = 3.0 * NEG_BIG
LOG2E = math.log2(math.e)

assert ROW_TILE == ATT_BK and ATT_BK % ATT_BQ == 0

VMEM_LIMIT = 56 * 1024 * 1024

F32 = jnp.float32
BF16 = jnp.bfloat16


def _cparams(sem):
    return pltpu.CompilerParams(dimension_semantics=sem, vmem_limit_bytes=VMEM_LIMIT)


def _rms(x, g):
    return x * lax.rsqrt(jnp.mean(x * x, axis=-1, keepdims=True) + RMS_EPS) * g


def _dot(a, b):
    return jnp.dot(a, b, preferred_element_type=F32)


def _dot_nt(a, b, precision=None):
    return lax.dot_general(a, b, (((1,), (1,)), ((), ())),
                           preferred_element_type=F32, precision=precision)


def _split3(x):
    hi = x.astype(BF16)
    r1 = x - hi.astype(F32)
    mid = r1.astype(BF16)
    lo = (r1 - mid.astype(F32)).astype(BF16)
    return hi, mid, lo


def _rmsnorm_kernel(h_ref, g_ref, o_ref):
    o_ref[...] = _rms(h_ref[...], g_ref[...])


def _rmsnorm(h2, g):
    m = h2.shape[0]
    return pl.pallas_call(
        _rmsnorm_kernel,
        grid=(m // ROW_TILE,),
        in_specs=[pl.BlockSpec((ROW_TILE, D_MODEL), lambda i: (i, 0)),
                  pl.BlockSpec((1, D_MODEL), lambda i: (0, 0))],
        out_specs=pl.BlockSpec((ROW_TILE, D_MODEL), lambda i: (i, 0)),
        out_shape=jax.ShapeDtypeStruct((m, D_MODEL), F32),
        compiler_params=_cparams(("parallel",)),
        name="rmsnorm",
    )(h2, g.reshape(1, D_MODEL))


def _swap_halves(x):
    return pltpu.roll(x, 64, axis=x.ndim - 1)


def _ssm_prep_kernel(ldt_ref, are_ref, aim_ref, bt_ref, cc_ref,
                     w2_ref, pt_ref, r_ref, a1_ref, a2_ref):
    lane1 = lax.broadcasted_iota(jnp.int32, (1, LANES), 1)
    lo1 = lane1 < SSM_STATE
    a_re = are_ref[0]
    a_im = aim_ref[0]
    dt = jnp.exp(ldt_ref[0])
    zr = a_re * dt
    zi = a_im * dt
    mag = jnp.exp(zr)
    lr = mag * jnp.cos(zi)
    li = mag * jnp.sin(zi)
    imag = jnp.exp(-zr)
    ir = imag * jnp.cos(zi)
    ii = -imag * jnp.sin(zi)

    sgn1 = jnp.where(lo1, -1.0, 1.0)

    def powers(count, nbits, wr, wi, descending=False):
        n = lax.broadcasted_iota(jnp.int32, (count, 1, LANES), 0)
        if descending:
            n = count - 1 - n
        lo3 = lax.broadcasted_iota(jnp.int32, (count, 1, LANES), 2) < SSM_STATE
        p = jnp.where(lo3, 1.0, 0.0).astype(F32)
        ps = jnp.where(lo3, 0.0, 1.0).astype(F32)
        for k in range(nbits):
            bit = ((n >> k) & 1) == 1
            wn = (sgn1 * wi)[None]
            p, ps = (jnp.where(bit, p * wr[None] + ps * wn, p),
                     jnp.where(bit, ps * wr[None] - p * wn, ps))
            wr, wi = wr * wr - wi * wi, 2.0 * wr * wi
        return p, ps

    def times_coef(table, coef, rows):
        p, ps = table
        cs = _swap_halves(coef)
        lo = lax.broadcasted_iota(jnp.int32, coef.shape, 1) < SSM_STATE
        prod = p * jnp.where(lo, coef, cs)[None] + ps * jnp.where(lo, -cs, coef)[None]
        return prod.reshape(rows, LANES)

    nr = lr - 1.0
    den = a_re * a_re + a_im * a_im
    cr = (nr * a_re + li * a_im) / den
    ci = (li * a_re - nr * a_im) / den
    bt = bt_ref[0]
    bbar = bt * cr + _swap_halves(bt) * (sgn1 * ci)
    cc = cc_ref[0]

    lo_big = lax.broadcasted_iota(jnp.int32, (CHUNK_W, LANES), 1) < SSM_STATE
    p_pos = powers(SSM_CHUNK, 6, lr, li)
    cbig = times_coef(p_pos, cc, CHUNK_W)
    wn = (sgn1 * li)[None]
    p_next = (p_pos[0] * lr[None] + p_pos[1] * wn, p_pos[1] * lr[None] - p_pos[0] * wn)
    cl1 = times_coef(p_next, cc, CHUNK_W)
    pt_ref[0] = jnp.where(lo_big, cl1, -cl1).astype(BF16)
    r_ref[0] = times_coef(powers(SSM_CHUNK, 6, lr, li, descending=True), bbar, CHUNK_W).astype(BF16)

    bsmall = times_coef(powers(LANES // SSM_GROUP, 3, ir, ii), bbar, LANES)
    lo8 = lax.broadcasted_iota(jnp.int32, (LANES, LANES), 1) < SSM_STATE
    lhs = jnp.where(lo8, bsmall, -bsmall)
    w2 = _dot_nt(lhs, cbig, precision=lax.Precision.HIGHEST)
    n8 = lax.broadcasted_iota(jnp.int32, (LANES, 1), 0) >> GROUP_SHIFT
    col_t = lax.broadcasted_iota(jnp.int32, (LANES, CHUNK_W), 1) >> GROUP_SHIFT
    w2_ref[0] = jnp.where(col_t >= n8, w2, 0.0).astype(BF16)

    ar, ai = lr, li
    for _ in range(6):
        ar, ai = ar * ar - ai * ai, 2.0 * ar * ai
    a1_ref[0] = ar
    a2_ref[0] = jnp.where(lo1, -ai, ai)


def _ssm_prep(log_dt, a_re, a_im, b_re, b_im, c_re, c_im):
    g = N_GROUPS
    dup = lambda x: jnp.concatenate([x, x], axis=-1).reshape(g, 1, LANES)
    ldt = jnp.broadcast_to(log_dt.reshape(g, 1, 1), (g, 1, LANES))
    bt = jnp.concatenate([b_re.transpose(0, 2, 1), b_im.transpose(0, 2, 1)], axis=-1)
    cc = jnp.concatenate([c_re, c_im], axis=-1)
    vec = pl.BlockSpec((1, 1, LANES), lambda i: (i, 0, 0))
    mat = pl.BlockSpec((1, SSM_GROUP, LANES), lambda i: (i, 0, 0))
    return pl.pallas_call(
        _ssm_prep_kernel,
        grid=(g,),
        in_specs=[vec, vec, vec, mat, mat],
        out_specs=[pl.BlockSpec((1, LANES, CHUNK_W), lambda i: (i, 0, 0)),
                   pl.BlockSpec((1, CHUNK_W, LANES), lambda i: (i, 0, 0)),
                   pl.BlockSpec((1, CHUNK_W, LANES), lambda i: (i, 0, 0)),
                   vec, vec],
        out_shape=[jax.ShapeDtypeStruct((g, LANES, CHUNK_W), BF16),
                   jax.ShapeDtypeStruct((g, CHUNK_W, LANES), BF16),
                   jax.ShapeDtypeStruct((g, CHUNK_W, LANES), BF16),
                   jax.ShapeDtypeStruct((g, 1, LANES), F32),
                   jax.ShapeDtypeStruct((g, 1, LANES), F32)],
        compiler_params=_cparams(("parallel",)),
        name="ssm_prep",
    )(ldt, dup(a_re), dup(a_im), bt, cc)


def _ssm_summary_kernel(u_ref, r_ref, v_ref):
    v_ref[...] = _dot(u_ref[0].astype(BF16), r_ref[0])


def _ssm_summary(u, r):
    g, m, _ = u.shape
    return pl.pallas_call(
        _ssm_summary_kernel,
        grid=(g,),
        in_specs=[pl.BlockSpec((1, m, CHUNK_W), lambda i: (i, 0, 0)),
                  pl.BlockSpec((1, CHUNK_W, LANES), lambda i: (i, 0, 0))],
        out_specs=pl.BlockSpec((m, LANES), lambda i: (0, i)),
        out_shape=jax.ShapeDtypeStruct((m, g * LANES), F32),
        compiler_params=_cparams(("parallel",)),
        name="ssm_summary",
    )(u, r)


def _ssm_scan_kernel(v_ref, a1_ref, a2_ref, s_ref):
    a1 = a1_ref[...]
    a2 = a2_ref[...]
    nk = v_ref.shape[1]

    def step(k, s):
        s_ref[0, k] = s
        return s * a1 + _swap_halves(s) * a2 + v_ref[0, k]

    lax.fori_loop(0, nk, step, jnp.zeros((N_GROUPS, LANES), F32))


def _ssm_scan(v4, a1, a2):
    b, nk = v4.shape[0], v4.shape[1]
    blk = pl.BlockSpec((1, nk, N_GROUPS, LANES), lambda i: (i, 0, 0, 0))
    coef = pl.BlockSpec((N_GROUPS, LANES), lambda i: (0, 0))
    return pl.pallas_call(
        _ssm_scan_kernel,
        grid=(b,),
        in_specs=[blk, coef, coef],
        out_specs=blk,
        out_shape=jax.ShapeDtypeStruct(v4.shape, F32),
        compiler_params=_cparams(("parallel",)),
        name="ssm_scan",
    )(v4, a1, a2)


def _ssm_output_kernel(u_ref, s_ref, w2_ref, pt_ref, d_ref, z_ref, toep_ref):
    @pl.when(pl.program_id(0) == 0)
    def _():
        toep_ref[...] = jnp.zeros_like(toep_ref)

    nblk = CHUNK_W // LANES
    for i in range(nblk):
        toep_ref[i * LANES:(i + 1) * LANES, i * LANES:] = w2_ref[0, :, :CHUNK_W - i * LANES]

    u = u_ref[0]
    ub = u.astype(BF16)
    y = _dot_nt(s_ref[...].astype(BF16), pt_ref[0]) + d_ref[0] * u
    tile = 256
    cols = []
    for j in range(CHUNK_W // tile):
        kk = (j + 1) * tile
        cols.append(_dot(ub[:, :kk], toep_ref[:kk, j * tile:(j + 1) * tile]))
    y = y + jnp.concatenate(cols, axis=1)
    c0 = math.sqrt(2.0 / math.pi)
    z_ref[0] = (0.5 * y * (1.0 + jnp.tanh(c0 * (y + 0.044715 * (y * y * y))))).astype(BF16)


def _ssm_output(u, s, w2, pt, d_t):
    g, m, _ = u.shape
    return pl.pallas_call(
        _ssm_output_kernel,
        grid=(g,),
        in_specs=[pl.BlockSpec((1, m, CHUNK_W), lambda i: (i, 0, 0)),
                  pl.BlockSpec((m, LANES), lambda i: (0, i)),
                  pl.BlockSpec((1, LANES, CHUNK_W), lambda i: (i, 0, 0)),
                  pl.BlockSpec((1, CHUNK_W, LANES), lambda i: (i, 0, 0)),
                  pl.BlockSpec((1, 1, CHUNK_W), lambda i: (i, 0, 0))],
        out_specs=pl.BlockSpec((1, m, CHUNK_W), lambda i: (i, 0, 0)),
        out_shape=jax.ShapeDtypeStruct((g, m, CHUNK_W), BF16),
        scratch_shapes=[pltpu.VMEM((CHUNK_W, CHUNK_W), BF16)],
        compiler_params=_cparams(("arbitrary",)),
        name="ssm_output",
    )(u, s, w2, pt, d_t)


def _glu_kernel(h_ref, z_ref, w_ref, o_ref):
    zw = _dot(z_ref[...], w_ref[...])
    o_ref[...] = h_ref[...] + zw[:, :D_MODEL] * jax.nn.sigmoid(zw[:, D_MODEL:])


def _proj_res_kernel(h_ref, z_ref, w_ref, o_ref):
    o_ref[...] = h_ref[...] + _dot(z_ref[...], w_ref[...])


def _row_call(kernel, name, h2, z2, w):
    m = h2.shape[0]
    return pl.pallas_call(
        kernel,
        grid=(m // ROW_TILE,),
        in_specs=[pl.BlockSpec((ROW_TILE, D_MODEL), lambda i: (i, 0)),
                  pl.BlockSpec((ROW_TILE, z2.shape[1]), lambda i: (i, 0)),
                  pl.BlockSpec(w.shape, lambda i: (0, 0))],
        out_specs=pl.BlockSpec((ROW_TILE, D_MODEL), lambda i: (i, 0)),
        out_shape=jax.ShapeDtypeStruct((m, D_MODEL), F32),
        compiler_params=_cparams(("parallel",)),
        name=name,
    )(h2, z2, w)


def _mlp_kernel(h_ref, g_ref, w1_ref, w2_ref, fg_ref, o_ref, *, final_norm):
    x = h_ref[...]
    xn = _rms(x, g_ref[...]).astype(BF16)
    acc = x
    for f in range(0, D_FF, FF_CHUNK):
        a = jnp.square(jnp.maximum(_dot(xn, w1_ref[:, f:f + FF_CHUNK]), 0.0))
        acc = acc + _dot(a.astype(BF16), w2_ref[f:f + FF_CHUNK, :])
    if final_norm:
        acc = _rms(acc, fg_ref[...])
    o_ref[...] = acc


def _mlp(h2, g, w1, w2, fg, final_norm):
    m = h2.shape[0]
    return pl.pallas_call(
        functools.partial(_mlp_kernel, final_norm=final_norm),
        grid=(m // ROW_TILE,),
        in_specs=[pl.BlockSpec((ROW_TILE, D_MODEL), lambda i: (i, 0)),
                  pl.BlockSpec((1, D_MODEL), lambda i: (0, 0)),
                  pl.BlockSpec((D_MODEL, D_FF), lambda i: (0, 0)),
                  pl.BlockSpec((D_FF, D_MODEL), lambda i: (0, 0)),
                  pl.BlockSpec((1, D_MODEL), lambda i: (0, 0))],
        out_specs=pl.BlockSpec((ROW_TILE, D_MODEL), lambda i: (i, 0)),
        out_shape=jax.ShapeDtypeStruct((m, D_MODEL), F32),
        compiler_params=_cparams(("parallel",)),
        name="mlp_final" if final_norm else "mlp",
    )(h2, g.reshape(1, D_MODEL), w1, w2, fg.reshape(1, D_MODEL))


def _extra_scatter(sign_f, f_first):
    scat = np.zeros((N_EXTRA, LANES, N_HEADS * HEAD_PAD), np.float32)
    const = np.zeros((1, N_HEADS * HEAD_PAD), np.float32)
    for h in range(N_HEADS):
        base = h * HEAD_PAD + HEAD_DIM
        f0, o0 = (0, N_EXTRA) if f_first else (N_EXTRA, 0)
        for j in range(N_EXTRA):
            scat[j, h, base + f0 + j] = sign_f
            const[0, base + o0 + j] = 1.0
    return jnp.asarray(scat, BF16), jnp.asarray(const, F32)


def _kv_kernel(h_ref, g_ref, wk_ref, wv_ref, vone_ref, wf_ref, bf_ref, sc_ref, cst_ref,
               k_ref, v_ref, f_ref, carry_ref):
    @pl.when(pl.program_id(1) == 0)
    def _():
        carry_ref[...] = jnp.zeros_like(carry_ref)

    xn = _rms(h_ref[0], g_ref[...]).astype(BF16)
    v_ref[0, 0] = (_dot_nt(wv_ref[...], xn) + vone_ref[...]).astype(BF16)

    logit = _dot(xn, wf_ref[...]) + bf_ref[...]
    log_f = jnp.minimum(logit, 0.0) - jnp.log1p(jnp.exp(-jnp.abs(logit)))
    lane = lax.broadcasted_iota(jnp.int32, log_f.shape, 1)
    log_f = jnp.where(lane < N_HEADS, log_f, 0.0)
    t = log_f.shape[0]
    tri = (lax.broadcasted_iota(jnp.int32, (t, t), 0)
           >= lax.broadcasted_iota(jnp.int32, (t, t), 1)).astype(BF16)
    hi, mid, lo = _split3(log_f)
    cum = (_dot(tri, hi) + _dot(tri, mid)) + _dot(tri, lo) + carry_ref[...]
    carry_ref[...] = cum[t - 1:t, :]
    f_ref[0] = cum

    fh, fm, fl = _split3(cum * LOG2E)
    kaug = (_dot(xn, wk_ref[...]) + cst_ref[...]
            + _dot(fh, sc_ref[0]) + _dot(fm, sc_ref[1]) + _dot(fl, sc_ref[2])).astype(BF16)
    for hh in range(N_HEADS):
        k_ref[0, hh] = kaug[:, hh * HEAD_PAD:(hh + 1) * HEAD_PAD]


def _kv_proj(h3, g, wk_aug, wv, wf, bf):
    b, l, _ = h3.shape
    nt = l // ROW_TILE
    scat, const = _extra_scatter(-1.0, f_first=False)
    vone = np.zeros((N_HEADS, VT_ROWS, 1), np.float32)
    vone[:, HEAD_DIM] = 1.0
    vone = jnp.asarray(vone.reshape(N_HEADS * VT_ROWS, 1))
    full = lambda a: pl.BlockSpec(a.shape, lambda i, j: (0,) * a.ndim)
    return pl.pallas_call(
        _kv_kernel,
        grid=(b, nt),
        in_specs=[pl.BlockSpec((1, ROW_TILE, D_MODEL), lambda i, j: (i, j, 0)),
                  pl.BlockSpec((1, D_MODEL), lambda i, j: (0, 0)),
                  full(wk_aug), full(wv), full(vone), full(wf), full(bf), full(scat), full(const)],
        out_specs=[pl.BlockSpec((1, N_HEADS, ROW_TILE, HEAD_PAD), lambda i, j: (i, 0, j, 0)),
                   pl.BlockSpec((1, 1, N_HEADS * VT_ROWS, ROW_TILE), lambda i, j: (i, j, 0, 0)),
                   pl.BlockSpec((1, ROW_TILE, LANES), lambda i, j: (i, j, 0))],
        out_shape=[jax.ShapeDtypeStruct((b, N_HEADS, l, HEAD_PAD), BF16),
                   jax.ShapeDtypeStruct((b, nt, N_HEADS * VT_ROWS, ROW_TILE), BF16),
                   jax.ShapeDtypeStruct((b, l, LANES), F32)],
        scratch_shapes=[pltpu.VMEM((1, LANES), F32)],
        compiler_params=_cparams(("parallel", "arbitrary")),
        name="kv_proj",
    )(h3, g.reshape(1, D_MODEL), wk_aug, wv, vone, wf, bf, scat, const)


def _q_kernel(h_ref, g_ref, wq_ref, f_ref, sc_ref, cst_ref, q_ref):
    xn = _rms(h_ref[0], g_ref[...]).astype(BF16)
    fh, fm, fl = _split3(f_ref[0] * LOG2E)
    qaug = (_dot(xn, wq_ref[...]) + cst_ref[...]
            + _dot(fh, sc_ref[0]) + _dot(fm, sc_ref[1]) + _dot(fl, sc_ref[2])).astype(BF16)
    for hh in range(N_HEADS):
        q_ref[0, hh] = qaug[:, hh * HEAD_PAD:(hh + 1) * HEAD_PAD]


def _q_proj(h3, g, wq_aug, fcum):
    b, l, _ = h3.shape
    scat, const = _extra_scatter(1.0, f_first=True)
    full = lambda a: pl.BlockSpec(a.shape, lambda i, j: (0,) * a.ndim)
    return pl.pallas_call(
        _q_kernel,
        grid=(b, l // ROW_TILE),
        in_specs=[pl.BlockSpec((1, ROW_TILE, D_MODEL), lambda i, j: (i, j, 0)),
                  pl.BlockSpec((1, D_MODEL), lambda i, j: (0, 0)),
                  full(wq_aug),
                  pl.BlockSpec((1, ROW_TILE, LANES), lambda i, j: (i, j, 0)),
                  full(scat), full(const)],
        out_specs=pl.BlockSpec((1, N_HEADS, ROW_TILE, HEAD_PAD), lambda i, j: (i, 0, j, 0)),
        out_shape=jax.ShapeDtypeStruct((b, N_HEADS, l, HEAD_PAD), BF16),
        compiler_params=_cparams(("parallel", "parallel")),
        name="q_proj",
    )(h3, g.reshape(1, D_MODEL), wq_aug, fcum, scat, const)


def _causal_items(length):
    return sum((qi * ATT_BQ) // ATT_BK + 1 for qi in range(length // ATT_BQ))


def _diag_bias():
    k = np.arange(ATT_BK)[:, None]
    q = np.arange(ATT_BQ)[None, :]
    tabs = [np.zeros((ATT_BK, ATT_BQ), np.float32)]
    for r in range(ATT_BK // ATT_BQ):
        tabs.append(np.where(k - q <= r * ATT_BQ, 0.0, NEG_BIG).astype(np.float32))
    return jnp.asarray(np.stack(tabs))


def _attn_kernel(q_ref, k_ref, vt_ref, bias_ref, o_ref, *scratch):
    s_scr = scratch[:ATT_SLOTS]
    p_scr = scratch[ATT_SLOTS:2 * ATT_SLOTS]
    acc_scr = scratch[2 * ATT_SLOTS]
    length = q_ref.shape[2]
    n_q = length // ATT_BQ
    n_items = _causal_items(length)
    assert n_items % ATT_SLOTS == 0

    def blocks_of(qi):
        return (qi * ATT_BQ) // ATT_BK + 1

    def advance(item):
        qi, j = item
        last = j + 1 == blocks_of(qi)
        return (jnp.minimum(jnp.where(last, qi + 1, qi), n_q - 1), jnp.where(last, 0, j + 1))

    def step(slot, item_s, item_x, item_v, chain, lane):
        m, acc = chain
        mb, alpha_q = lane
        out_chain, out_lane = [], []
        qi_v, j_v = item_v
        for hh in range(ATT_HEADS):
            vt = vt_ref[0, j_v, pl.ds(hh * VT_ROWS, VT_ROWS), :]
            acc_h = alpha_q[hh] * acc[hh] + _dot(vt, p_scr[slot][hh])
            acc_scr[qi_v, hh] = acc_h
            out_chain.append([None, acc_h])
        _, j_x = item_x
        for hh in range(ATT_HEADS):
            m_in = jnp.where(j_x == 0, NEG_BIG, m[hh])
            m_new = jnp.maximum(m_in, mb[hh])
            p_scr[slot][hh] = jnp.exp2(s_scr[slot][hh] - m_new).astype(BF16)
            out_chain[hh][0] = m_new
            out_lane.append([None, jnp.exp2(m_in - m_new)])
        qi_s, j_s = item_s
        shift = qi_s * ATT_BQ - j_s * ATT_BK
        sel = jnp.where(j_s + 1 == blocks_of(qi_s), 1 + shift // ATT_BQ, 0)
        for hh in range(ATT_HEADS):
            s = _dot_nt(k_ref[0, hh, pl.ds(j_s * ATT_BK, ATT_BK), :],
                        q_ref[0, hh, pl.ds(qi_s * ATT_BQ, ATT_BQ), :]) + bias_ref[sel]
            s_scr[slot][hh] = s
            out_lane[hh][0] = jnp.max(s, axis=0, keepdims=True)
        chain = tuple(tuple(c[i] for c in out_chain) for i in range(2))
        lane = tuple(tuple(c[i] for c in out_lane) for i in range(2))
        return chain, lane

    for slot in range(ATT_SLOTS):
        s_scr[slot][...] = jnp.full(s_scr[slot].shape, NULL_KEY, F32)
        p_scr[slot][...] = jnp.zeros(p_scr[slot].shape, BF16)

    def per_head(shape, v):
        pos = lax.broadcasted_iota(jnp.int32, shape, 0) + lax.broadcasted_iota(jnp.int32, shape, 1)
        return tuple(jnp.where(pos >= 0, v, 0.0).astype(F32) for _ in range(ATT_HEADS))

    row = (1, ATT_BQ)
    chain = (per_head(row, NEG_BIG), per_head((VT_ROWS, ATT_BQ), 0.0))
    lane = (per_head(row, NULL_KEY), per_head(row, 1.0))
    zero = jnp.int32(0)
    first = (zero, zero)

    def body(_, carry):
        item, hists, chain, lanes = carry
        new_hists, new_lanes = [], []
        for slot in range(ATT_SLOTS):
            item_x, item_v = hists[slot]
            chain, lane = step(slot, item, item_x, item_v, chain, lanes[slot])
            new_hists.append((item, item_x))
            new_lanes.append(lane)
            item = advance(item)
        return item, tuple(new_hists), chain, tuple(new_lanes)

    lax.fori_loop(0, n_items // ATT_SLOTS + 2, body,
                  (first, ((first, first),) * ATT_SLOTS, chain, (lane,) * ATT_SLOTS))

    def finish(qi, _):
        outs = []
        for hh in range(ATT_HEADS):
            a = acc_scr[qi, hh]
            outs.append(a[:HEAD_DIM] / a[HEAD_DIM:HEAD_DIM + 1])
        o_ref[0, pl.ds(qi * ATT_BQ, ATT_BQ), :] = jnp.concatenate(outs, axis=0).T.astype(BF16)
        return 0

    lax.fori_loop(0, n_q, finish, 0)


def _attention(q_aug, k_aug, vt):
    b, _, l, _ = q_aug.shape
    bias = _diag_bias()
    return pl.pallas_call(
        _attn_kernel,
        grid=(b, N_HEADS // ATT_HEADS),
        in_specs=[pl.BlockSpec((1, ATT_HEADS, l, HEAD_PAD), lambda i, h: (i, h, 0, 0)),
                  pl.BlockSpec((1, ATT_HEADS, l, HEAD_PAD), lambda i, h: (i, h, 0, 0)),
                  pl.BlockSpec((1, l // ATT_BK, ATT_HEADS * VT_ROWS, ATT_BK), lambda i, h: (i, 0, h, 0)),
                  pl.BlockSpec(bias.shape, lambda i, h: (0, 0, 0))],
        out_specs=pl.BlockSpec((1, l, ATT_HEADS * HEAD_DIM), lambda i, h: (i, 0, h)),
        out_shape=jax.ShapeDtypeStruct((b, l, D_MODEL), BF16),
        scratch_shapes=[pltpu.VMEM((ATT_HEADS, ATT_BK, ATT_BQ), F32)] * ATT_SLOTS
                       + [pltpu.VMEM((ATT_HEADS, ATT_BK, ATT_BQ), BF16)] * ATT_SLOTS
                       + [pltpu.VMEM((l // ATT_BQ, ATT_HEADS, VT_ROWS, ATT_BQ), F32)],
        compiler_params=_cparams(("parallel", "parallel")),
        name="fox_attention",
    )(q_aug, k_aug, vt, bias)


def _pad_heads(w):
    w3 = w.reshape(D_MODEL, N_HEADS, HEAD_DIM)
    w3 = jnp.pad(w3, ((0, 0), (0, 0), (0, HEAD_PAD - HEAD_DIM)))
    return w3.reshape(D_MODEL, N_HEADS * HEAD_PAD)


def kernel(x, mix_norm, mlp_norm, mlp_w1, mlp_w2, ssm_log_dt, ssm_a_re, ssm_a_im,
           ssm_b_re, ssm_b_im, ssm_c_re, ssm_c_im, ssm_d, ssm_w_glu, kv_norm, w_kvf, b_f,
           attn_wq, attn_wo, final_norm):
    bsz, length, _ = x.shape
    m = bsz * length
    nk = length // SSM_CHUNK
    h = x.reshape(m, D_MODEL)

    for i in range(DEPTH):
        if i < N_A_LAYERS:
            w2, pt, r, a1, a2 = _ssm_prep(ssm_log_dt[i], ssm_a_re[i], ssm_a_im[i], ssm_b_re[i],
                                          ssm_b_im[i], ssm_c_re[i], ssm_c_im[i])
            hn = _rmsnorm(h, mix_norm[i])
            u = hn.reshape(bsz * nk, SSM_CHUNK, N_GROUPS, SSM_GROUP).transpose(2, 0, 1, 3)
            u = u.reshape(N_GROUPS, bsz * nk, CHUNK_W)
            v = _ssm_summary(u, r)
            s = _ssm_scan(v.reshape(bsz, nk, N_GROUPS, LANES),
                          a1.reshape(N_GROUPS, LANES), a2.reshape(N_GROUPS, LANES))
            d_t = jnp.tile(ssm_d[i].reshape(N_GROUPS, 1, SSM_GROUP), (1, 1, SSM_CHUNK))
            z = _ssm_output(u, s.reshape(bsz * nk, N_GROUPS * LANES), w2, pt, d_t)
            z = z.reshape(N_GROUPS, bsz * nk, SSM_CHUNK, SSM_GROUP).transpose(1, 2, 0, 3)
            h = _row_call(_glu_kernel, "glu", h, z.reshape(m, D_MODEL), ssm_w_glu[i].astype(BF16))
        else:
            j = i - N_A_LAYERS
            wq_aug = _pad_heads(attn_wq[j] * (HEAD_DIM ** -0.5 * LOG2E)).astype(BF16)
            q_aug = _q_proj(h.reshape(bsz, length, D_MODEL), mix_norm[i], wq_aug, fcum)
            o = _attention(q_aug, k_aug, v_nat)
            h = _row_call(_proj_res_kernel, "attn_out", h, o.reshape(m, D_MODEL),
                          attn_wo[j].astype(BF16))
        h = _mlp(h, mlp_norm[i], mlp_w1[i].astype(BF16), mlp_w2[i].astype(BF16),
                 final_norm, final_norm=(i == DEPTH - 1))
        if i == N_A_LAYERS - 1:
            attn_dim = N_HEADS * HEAD_DIM
            wk_aug = _pad_heads(w_kvf[:, :attn_dim]).astype(BF16)
            wv = w_kvf[:, attn_dim:2 * attn_dim].T.reshape(N_HEADS, HEAD_DIM, D_MODEL)
            wv = jnp.pad(wv, ((0, 0), (0, VT_ROWS - HEAD_DIM), (0, 0)))
            wv = wv.reshape(N_HEADS * VT_ROWS, D_MODEL).astype(BF16)
            wf = jnp.pad(w_kvf[:, 2 * attn_dim:], ((0, 0), (0, LANES - N_HEADS))).astype(BF16)
            bf = jnp.pad(b_f, (0, LANES - N_HEADS)).reshape(1, LANES)
            k_aug, v_nat, fcum = _kv_proj(h.reshape(bsz, length, D_MODEL), kv_norm,
                                          wk_aug, wv, wf, bf)
    return h.reshape(bsz, length, D_MODEL)
```

```python
import functools
import math

import numpy as np
import jax
import jax.numpy as jnp
from jax import lax
from jax.experimental import pallas as pl
from jax.experimental.pallas import tpu as pltpu

D_MODEL = 1024
N_GROUPS = 64
SSM_GROUP = 16
GROUP_SHIFT = 4
SSM_STATE = 64
N_HEADS = 16
HEAD_DIM = 64
D_FF = 4 * D_MODEL
RMS_EPS = 1e-6
N_A_LAYERS = 2
DEPTH = 4

SSM_CHUNK = 64
CHUNK_W = SSM_CHUNK * SSM_GROUP
LANES = 128
HEAD_PAD = 128
N_EXTRA = 3

ROW_TILE = 512
FF_CHUNK = 1024
ATT_BQ = 256
VT_ROWS = 80
ATT_SLOTS = 8
ATT_HEADS = 2
ATT_BK = 512
NEG_BIG = -1e30
NULL_KEY = 3.0 * NEG_BIG
LOG2E = math.log2(math.e)

assert ROW_TILE == ATT_BK and ATT_BK % ATT_BQ == 0

VMEM_LIMIT = 56 * 1024 * 1024

F32 = jnp.float32
BF16 = jnp.bfloat16


def _cparams(sem):
    return pltpu.CompilerParams(dimension_semantics=sem, vmem_limit_bytes=VMEM_LIMIT)


def _rms(x, g):
    return x * lax.rsqrt(jnp.mean(x * x, axis=-1, keepdims=True) + RMS_EPS) * g


def _dot(a, b):
    return jnp.dot(a, b, preferred_element_type=F32)


def _dot_nt(a, b, precision=None):
    return lax.dot_general(a, b, (((1,), (1,)), ((), ())),
                           preferred_element_type=F32, precision=precision)


def _split3(x):
    hi = x.astype(BF16)
    r1 = x - hi.astype(F32)
    mid = r1.astype(BF16)
    lo = (r1 - mid.astype(F32)).astype(BF16)
    return hi, mid, lo


PACK_CHUNKS = 8
PIECES = LANES // SSM_GROUP
assert PACK_CHUNKS * SSM_CHUNK == ROW_TILE and PIECES == 8


def _piece_transpose(blocks):
    piece = lax.broadcasted_iota(jnp.int32, (PACK_CHUNKS, LANES), 1) >> GROUP_SHIFT
    for d in (4, 2, 1):
        keep = (piece & d) == 0
        nxt = []
        for v in blocks:
            out = list(v)
            for a in range(PIECES):
                if a & d:
                    continue
                b = a + d
                out[a] = jnp.where(keep, v[a], pltpu.roll(v[b], SSM_GROUP * d, axis=1))
                out[b] = jnp.where(keep, pltpu.roll(v[a], LANES - SSM_GROUP * d, axis=1), v[b])
            nxt.append(out)
        blocks = nxt
    return blocks


LANE_BLOCKS = D_MODEL // LANES


def _pack_chunks(tok_ref, grp_ref):
    n_sb = SSM_CHUNK // PIECES
    for gb in range(LANE_BLOCKS):
        v = [[tok_ref[gb, pl.ds(PIECES * sb + j, PACK_CHUNKS, stride=SSM_CHUNK), :]
              for j in range(PIECES)] for sb in range(n_sb)]
        w = _piece_transpose(v)
        for sb in range(n_sb):
            for i in range(PIECES):
                grp_ref[PIECES * gb + i, :, sb * LANES:(sb + 1) * LANES] = w[sb][i]


def _unpack_chunks(grp_ref, tok_ref):
    n_sb = SSM_CHUNK // PIECES
    for gb in range(LANE_BLOCKS):
        w = [[grp_ref[PIECES * gb + i, :, sb * LANES:(sb + 1) * LANES] for i in range(PIECES)]
             for sb in range(n_sb)]
        v = _piece_transpose(w)
        for sb in range(n_sb):
            for j in range(PIECES):
                tok_ref[gb, pl.ds(PIECES * sb + j, PACK_CHUNKS, stride=SSM_CHUNK), :] = v[sb][j]


def _norm_pack_kernel(h_ref, g_ref, u_ref, xn_scr):
    xn = _rms(h_ref[...], g_ref[...])
    for gb in range(LANE_BLOCKS):
        xn_scr[gb] = xn[:, gb * LANES:(gb + 1) * LANES]
    _pack_chunks(xn_scr, u_ref)


def _norm_pack(h2, g):
    m = h2.shape[0]
    return pl.pallas_call(
        _norm_pack_kernel,
        grid=(m // ROW_TILE,),
        in_specs=[pl.BlockSpec((ROW_TILE, D_MODEL), lambda i: (i, 0)),
                  pl.BlockSpec((1, D_MODEL), lambda i: (0, 0))],
        out_specs=pl.BlockSpec((N_GROUPS, PACK_CHUNKS, CHUNK_W), lambda i: (0, i, 0)),
        out_shape=jax.ShapeDtypeStruct((N_GROUPS, m // SSM_CHUNK, CHUNK_W), F32),
        scratch_shapes=[pltpu.VMEM((LANE_BLOCKS, ROW_TILE, LANES), F32)],
        compiler_params=_cparams(("parallel",)),
        name="norm_pack",
    )(h2, g.reshape(1, D_MODEL))


def _swap_halves(x):
    return pltpu.roll(x, 64, axis=x.ndim - 1)


def _ssm_prep_kernel(ldt_ref, are_ref, aim_ref, bt_ref, cc_ref,
                     w2_ref, pt_ref, r_ref, a1_ref, a2_ref):
    lane1 = lax.broadcasted_iota(jnp.int32, (1, LANES), 1)
    lo1 = lane1 < SSM_STATE
    a_re = are_ref[0]
    a_im = aim_ref[0]
    dt = jnp.exp(ldt_ref[0])
    zr = a_re * dt
    zi = a_im * dt
    mag = jnp.exp(zr)
    lr = mag * jnp.cos(zi)
    li = mag * jnp.sin(zi)
    imag = jnp.exp(-zr)
    ir = imag * jnp.cos(zi)
    ii = -imag * jnp.sin(zi)

    sgn1 = jnp.where(lo1, -1.0, 1.0)

    def powers(count, nbits, wr, wi, descending=False):
        n = lax.broadcasted_iota(jnp.int32, (count, 1, LANES), 0)
        if descending:
            n = count - 1 - n
        lo3 = lax.broadcasted_iota(jnp.int32, (count, 1, LANES), 2) < SSM_STATE
        p = jnp.where(lo3, 1.0, 0.0).astype(F32)
        ps = jnp.where(lo3, 0.0, 1.0).astype(F32)
        for k in range(nbits):
            bit = ((n >> k) & 1) == 1
            wn = (sgn1 * wi)[None]
            p, ps = (jnp.where(bit, p * wr[None] + ps * wn, p),
                     jnp.where(bit, ps * wr[None] - p * wn, ps))
            wr, wi = wr * wr - wi * wi, 2.0 * wr * wi
        return p, ps

    def times_coef(table, coef, rows):
        p, ps = table
        cs = _swap_halves(coef)
        lo = lax.broadcasted_iota(jnp.int32, coef.shape, 1) < SSM_STATE
        prod = p * jnp.where(lo, coef, cs)[None] + ps * jnp.where(lo, -cs, coef)[None]
        return prod.reshape(rows, LANES)

    nr = lr - 1.0
    den = a_re * a_re + a_im * a_im
    cr = (nr * a_re + li * a_im) / den
    ci = (li * a_re - nr * a_im) / den
    bt = bt_ref[0]
    bbar = bt * cr + _swap_halves(bt) * (sgn1 * ci)
    cc = cc_ref[0]

    lo_big = lax.broadcasted_iota(jnp.int32, (CHUNK_W, LANES), 1) < SSM_STATE
    p_pos = powers(SSM_CHUNK, 6, lr, li)
    cbig = times_coef(p_pos, cc, CHUNK_W)
    wn = (sgn1 * li)[None]
    p_next = (p_pos[0] * lr[None] + p_pos[1] * wn, p_pos[1] * lr[None] - p_pos[0] * wn)
    cl1 = times_coef(p_next, cc, CHUNK_W)
    pt_ref[0] = jnp.where(lo_big, cl1, -cl1).astype(BF16)
    r_ref[0] = times_coef(powers(SSM_CHUNK, 6, lr, li, descending=True), bbar, CHUNK_W).astype(BF16)

    bsmall = times_coef(powers(LANES // SSM_GROUP, 3, ir, ii), bbar, LANES)
    lo8 = lax.broadcasted_iota(jnp.int32, (LANES, LANES), 1) < SSM_STATE
    lhs = jnp.where(lo8, bsmall, -bsmall)
    w2 = _dot_nt(lhs, cbig, precision=lax.Precision.HIGHEST)
    n8 = lax.broadcasted_iota(jnp.int32, (LANES, 1), 0) >> GROUP_SHIFT
    col_t = lax.broadcasted_iota(jnp.int32, (LANES, CHUNK_W), 1) >> GROUP_SHIFT
    w2_ref[0] = jnp.where(col_t >= n8, w2, 0.0).astype(BF16)

    ar, ai = lr, li
    for _ in range(6):
        ar, ai = ar * ar - ai * ai, 2.0 * ar * ai
    a1_ref[0] = ar
    a2_ref[0] = jnp.where(lo1, -ai, ai)


def _ssm_prep(log_dt, a_re, a_im, b_re, b_im, c_re, c_im):
    g = N_GROUPS
    dup = lambda x: jnp.concatenate([x, x], axis=-1).reshape(g, 1, LANES)
    ldt = jnp.broadcast_to(log_dt.reshape(g, 1, 1), (g, 1, LANES))
    bt = jnp.concatenate([b_re.transpose(0, 2, 1), b_im.transpose(0, 2, 1)], axis=-1)
    cc = jnp.concatenate([c_re, c_im], axis=-1)
    vec = pl.BlockSpec((1, 1, LANES), lambda i: (i, 0, 0))
    mat = pl.BlockSpec((1, SSM_GROUP, LANES), lambda i: (i, 0, 0))
    return pl.pallas_call(
        _ssm_prep_kernel,
        grid=(g,),
        in_specs=[vec, vec, vec, mat, mat],
        out_specs=[pl.BlockSpec((1, LANES, CHUNK_W), lambda i: (i, 0, 0)),
                   pl.BlockSpec((1, CHUNK_W, LANES), lambda i: (i, 0, 0)),
                   pl.BlockSpec((1, CHUNK_W, LANES), lambda i: (i, 0, 0)),
                   vec, vec],
        out_shape=[jax.ShapeDtypeStruct((g, LANES, CHUNK_W), BF16),
                   jax.ShapeDtypeStruct((g, CHUNK_W, LANES), BF16),
                   jax.ShapeDtypeStruct((g, CHUNK_W, LANES), BF16),
                   jax.ShapeDtypeStruct((g, 1, LANES), F32),
                   jax.ShapeDtypeStruct((g, 1, LANES), F32)],
        compiler_params=_cparams(("parallel",)),
        name="ssm_prep",
    )(ldt, dup(a_re), dup(a_im), bt, cc)


def _ssm_summary_kernel(u_ref, r_ref, v_ref):
    v_ref[...] = _dot(u_ref[0].astype(BF16), r_ref[0])


def _ssm_summary(u, r):
    g, m, _ = u.shape
    return pl.pallas_call(
        _ssm_summary_kernel,
        grid=(g,),
        in_specs=[pl.BlockSpec((1, m, CHUNK_W), lambda i: (i, 0, 0)),
                  pl.BlockSpec((1, CHUNK_W, LANES), lambda i: (i, 0, 0))],
        out_specs=pl.BlockSpec((m, LANES), lambda i: (0, i)),
        out_shape=jax.ShapeDtypeStruct((m, g * LANES), F32),
        compiler_params=_cparams(("parallel",)),
        name="ssm_summary",
    )(u, r)


def _ssm_scan_kernel(v_ref, a1_ref, a2_ref, s_ref):
    a1 = a1_ref[...]
    a2 = a2_ref[...]
    nk = v_ref.shape[1]

    def step(k, s):
        s_ref[0, k] = s
        return s * a1 + _swap_halves(s) * a2 + v_ref[0, k]

    lax.fori_loop(0, nk, step, jnp.zeros((N_GROUPS, LANES), F32))


def _ssm_scan(v4, a1, a2):
    b, nk = v4.shape[0], v4.shape[1]
    blk = pl.BlockSpec((1, nk, N_GROUPS, LANES), lambda i: (i, 0, 0, 0))
    coef = pl.BlockSpec((N_GROUPS, LANES), lambda i: (0, 0))
    return pl.pallas_call(
        _ssm_scan_kernel,
        grid=(b,),
        in_specs=[blk, coef, coef],
        out_specs=blk,
        out_shape=jax.ShapeDtypeStruct(v4.shape, F32),
        compiler_params=_cparams(("parallel",)),
        name="ssm_scan",
    )(v4, a1, a2)


def _ssm_output_kernel(u_ref, s_ref, w2_ref, pt_ref, d_ref, z_ref, toep_ref):
    @pl.when(pl.program_id(0) == 0)
    def _():
        toep_ref[...] = jnp.zeros_like(toep_ref)

    nblk = CHUNK_W // LANES
    for i in range(nblk):
        toep_ref[i * LANES:(i + 1) * LANES, i * LANES:] = w2_ref[0, :, :CHUNK_W - i * LANES]

    u = u_ref[0]
    ub = u.astype(BF16)
    y = _dot_nt(s_ref[...].astype(BF16), pt_ref[0]) + d_ref[0] * u
    tile = 256
    cols = []
    for j in range(CHUNK_W // tile):
        kk = (j + 1) * tile
        cols.append(_dot(ub[:, :kk], toep_ref[:kk, j * tile:(j + 1) * tile]))
    y = y + jnp.concatenate(cols, axis=1)
    c0 = math.sqrt(2.0 / math.pi)
    z_ref[0] = 0.5 * y * (1.0 + jnp.tanh(c0 * (y + 0.044715 * (y * y * y))))


def _ssm_output(u, s, w2, pt, d_t):
    g, m, _ = u.shape
    return pl.pallas_call(
        _ssm_output_kernel,
        grid=(g,),
        in_specs=[pl.BlockSpec((1, m, CHUNK_W), lambda i: (i, 0, 0)),
                  pl.BlockSpec((m, LANES), lambda i: (0, i)),
                  pl.BlockSpec((1, LANES, CHUNK_W), lambda i: (i, 0, 0)),
                  pl.BlockSpec((1, CHUNK_W, LANES), lambda i: (i, 0, 0)),
                  pl.BlockSpec((1, 1, CHUNK_W), lambda i: (i, 0, 0))],
        out_specs=pl.BlockSpec((1, m, CHUNK_W), lambda i: (i, 0, 0)),
        out_shape=jax.ShapeDtypeStruct((g, m, CHUNK_W), F32),
        scratch_shapes=[pltpu.VMEM((CHUNK_W, CHUNK_W), BF16)],
        compiler_params=_cparams(("arbitrary",)),
        name="ssm_output",
    )(u, s, w2, pt, d_t)


def _glu_kernel(h_ref, z_ref, w_ref, o_ref, z_scr):
    _unpack_chunks(z_ref, z_scr)
    z = jnp.concatenate([z_scr[gb].astype(BF16) for gb in range(LANE_BLOCKS)], axis=1)
    zw = _dot(z, w_ref[...])
    o_ref[...] = h_ref[...] + zw[:, :D_MODEL] * jax.nn.sigmoid(zw[:, D_MODEL:])


def _glu(h2, z, w):
    m = h2.shape[0]
    return pl.pallas_call(
        _glu_kernel,
        grid=(m // ROW_TILE,),
        in_specs=[pl.BlockSpec((ROW_TILE, D_MODEL), lambda i: (i, 0)),
                  pl.BlockSpec((N_GROUPS, PACK_CHUNKS, CHUNK_W), lambda i: (0, i, 0)),
                  pl.BlockSpec(w.shape, lambda i: (0, 0))],
        out_specs=pl.BlockSpec((ROW_TILE, D_MODEL), lambda i: (i, 0)),
        out_shape=jax.ShapeDtypeStruct((m, D_MODEL), F32),
        scratch_shapes=[pltpu.VMEM((LANE_BLOCKS, ROW_TILE, LANES), F32)],
        compiler_params=_cparams(("parallel",)),
        name="glu",
    )(h2, z, w)


def _proj_res_kernel(h_ref, z_ref, w_ref, o_ref):
    o_ref[...] = h_ref[...] + _dot(z_ref[...], w_ref[...])


def _proj_res(h2, z2, w):
    m = h2.shape[0]
    return pl.pallas_call(
        _proj_res_kernel,
        grid=(m // ROW_TILE,),
        in_specs=[pl.BlockSpec((ROW_TILE, D_MODEL), lambda i: (i, 0)),
                  pl.BlockSpec((ROW_TILE, z2.shape[1]), lambda i: (i, 0)),
                  pl.BlockSpec(w.shape, lambda i: (0, 0))],
        out_specs=pl.BlockSpec((ROW_TILE, D_MODEL), lambda i: (i, 0)),
        out_shape=jax.ShapeDtypeStruct((m, D_MODEL), F32),
        compiler_params=_cparams(("parallel",)),
        name="attn_out",
    )(h2, z2, w)


def _mlp_kernel(h_ref, g_ref, w1_ref, w2_ref, fg_ref, o_ref, *, final_norm):
    x = h_ref[...]
    xn = _rms(x, g_ref[...]).astype(BF16)
    acc = x
    for f in range(0, D_FF, FF_CHUNK):
        a = jnp.square(jnp.maximum(_dot(xn, w1_ref[:, f:f + FF_CHUNK]), 0.0))
        acc = acc + _dot(a.astype(BF16), w2_ref[f:f + FF_CHUNK, :])
    if final_norm:
        acc = _rms(acc, fg_ref[...])
    o_ref[...] = acc


def _mlp(h2, g, w1, w2, fg, final_norm):
    m = h2.shape[0]
    return pl.pallas_call(
        functools.partial(_mlp_kernel, final_norm=final_norm),
        grid=(m // ROW_TILE,),
        in_specs=[pl.BlockSpec((ROW_TILE, D_MODEL), lambda i: (i, 0)),
                  pl.BlockSpec((1, D_MODEL), lambda i: (0, 0)),
                  pl.BlockSpec((D_MODEL, D_FF), lambda i: (0, 0)),
                  pl.BlockSpec((D_FF, D_MODEL), lambda i: (0, 0)),
                  pl.BlockSpec((1, D_MODEL), lambda i: (0, 0))],
        out_specs=pl.BlockSpec((ROW_TILE, D_MODEL), lambda i: (i, 0)),
        out_shape=jax.ShapeDtypeStruct((m, D_MODEL), F32),
        compiler_params=_cparams(("parallel",)),
        name="mlp_final" if final_norm else "mlp",
    )(h2, g.reshape(1, D_MODEL), w1, w2, fg.reshape(1, D_MODEL))


def _extra_scatter(sign_f, f_first):
    scat = np.zeros((N_EXTRA, LANES, N_HEADS * HEAD_PAD), np.float32)
    const = np.zeros((1, N_HEADS * HEAD_PAD), np.float32)
    for h in range(N_HEADS):
        base = h * HEAD_PAD + HEAD_DIM
        f0, o0 = (0, N_EXTRA) if f_first else (N_EXTRA, 0)
        for j in range(N_EXTRA):
            scat[j, h, base + f0 + j] = sign_f
            const[0, base + o0 + j] = 1.0
    return jnp.asarray(scat, BF16), jnp.asarray(const, F32)


def _kv_kernel(h_ref, g_ref, wk_ref, wv_ref, vone_ref, wf_ref, bf_ref, sc_ref, cst_ref,
               k_ref, v_ref, f_ref, carry_ref):
    @pl.when(pl.program_id(1) == 0)
    def _():
        carry_ref[...] = jnp.zeros_like(carry_ref)

    xn = _rms(h_ref[0], g_ref[...]).astype(BF16)
    v_ref[0, 0] = (_dot_nt(wv_ref[...], xn) + vone_ref[...]).astype(BF16)

    logit = _dot(xn, wf_ref[...]) + bf_ref[...]
    log_f = jnp.minimum(logit, 0.0) - jnp.log1p(jnp.exp(-jnp.abs(logit)))
    lane = lax.broadcasted_iota(jnp.int32, log_f.shape, 1)
    log_f = jnp.where(lane < N_HEADS, log_f, 0.0)
    t = log_f.shape[0]
    tri = (lax.broadcasted_iota(jnp.int32, (t, t), 0)
           >= lax.broadcasted_iota(jnp.int32, (t, t), 1)).astype(BF16)
    hi, mid, lo = _split3(log_f)
    cum = (_dot(tri, hi) + _dot(tri, mid)) + _dot(tri, lo) + carry_ref[...]
    carry_ref[...] = cum[t - 1:t, :]
    f_ref[0] = cum

    fh, fm, fl = _split3(cum * LOG2E)
    kaug = (_dot(xn, wk_ref[...]) + cst_ref[...]
            + _dot(fh, sc_ref[0]) + _dot(fm, sc_ref[1]) + _dot(fl, sc_ref[2])).astype(BF16)
    for hh in range(N_HEADS):
        k_ref[0, hh] = kaug[:, hh * HEAD_PAD:(hh + 1) * HEAD_PAD]


def _kv_proj(h3, g, wk_aug, wv, wf, bf):
    b, l, _ = h3.shape
    nt = l // ROW_TILE
    scat, const = _extra_scatter(-1.0, f_first=False)
    vone = np.zeros((N_HEADS, VT_ROWS, 1), np.float32)
    vone[:, HEAD_DIM] = 1.0
    vone = jnp.asarray(vone.reshape(N_HEADS * VT_ROWS, 1))
    full = lambda a: pl.BlockSpec(a.shape, lambda i, j: (0,) * a.ndim)
    return pl.pallas_call(
        _kv_kernel,
        grid=(b, nt),
        in_specs=[pl.BlockSpec((1, ROW_TILE, D_MODEL), lambda i, j: (i, j, 0)),
                  pl.BlockSpec((1, D_MODEL), lambda i, j: (0, 0)),
                  full(wk_aug), full(wv), full(vone), full(wf), full(bf), full(scat), full(const)],
        out_specs=[pl.BlockSpec((1, N_HEADS, ROW_TILE, HEAD_PAD), lambda i, j: (i, 0, j, 0)),
                   pl.BlockSpec((1, 1, N_HEADS * VT_ROWS, ROW_TILE), lambda i, j: (i, j, 0, 0)),
                   pl.BlockSpec((1, ROW_TILE, LANES), lambda i, j: (i, j, 0))],
        out_shape=[jax.ShapeDtypeStruct((b, N_HEADS, l, HEAD_PAD), BF16),
                   jax.ShapeDtypeStruct((b, nt, N_HEADS * VT_ROWS, ROW_TILE), BF16),
                   jax.ShapeDtypeStruct((b, l, LANES), F32)],
        scratch_shapes=[pltpu.VMEM((1, LANES), F32)],
        compiler_params=_cparams(("parallel", "arbitrary")),
        name="kv_proj",
    )(h3, g.reshape(1, D_MODEL), wk_aug, wv, vone, wf, bf, scat, const)


def _q_kernel(h_ref, g_ref, wq_ref, f_ref, sc_ref, cst_ref, q_ref):
    xn = _rms(h_ref[0], g_ref[...]).astype(BF16)
    fh, fm, fl = _split3(f_ref[0] * LOG2E)
    qaug = (_dot(xn, wq_ref[...]) + cst_ref[...]
            + _dot(fh, sc_ref[0]) + _dot(fm, sc_ref[1]) + _dot(fl, sc_ref[2])).astype(BF16)
    for hh in range(N_HEADS):
        q_ref[0, hh] = qaug[:, hh * HEAD_PAD:(hh + 1) * HEAD_PAD]


def _q_proj(h3, g, wq_aug, fcum):
    b, l, _ = h3.shape
    scat, const = _extra_scatter(1.0, f_first=True)
    full = lambda a: pl.BlockSpec(a.shape, lambda i, j: (0,) * a.ndim)
    return pl.pallas_call(
        _q_kernel,
        grid=(b, l // ROW_TILE),
        in_specs=[pl.BlockSpec((1, ROW_TILE, D_MODEL), lambda i, j: (i, j, 0)),
                  pl.BlockSpec((1, D_MODEL), lambda i, j: (0, 0)),
                  full(wq_aug),
                  pl.BlockSpec((1, ROW_TILE, LANES), lambda i, j: (i, j, 0)),
                  full(scat), full(const)],
        out_specs=pl.BlockSpec((1, N_HEADS, ROW_TILE, HEAD_PAD), lambda i, j: (i, 0, j, 0)),
        out_shape=jax.ShapeDtypeStruct((b, N_HEADS, l, HEAD_PAD), BF16),
        compiler_params=_cparams(("parallel", "parallel")),
        name="q_proj",
    )(h3, g.reshape(1, D_MODEL), wq_aug, fcum, scat, const)


def _causal_items(length):
    return sum((qi * ATT_BQ) // ATT_BK + 1 for qi in range(length // ATT_BQ))


def _diag_bias():
    k = np.arange(ATT_BK)[:, None]
    q = np.arange(ATT_BQ)[None, :]
    tabs = [np.zeros((ATT_BK, ATT_BQ), np.float32)]
    for r in range(ATT_BK // ATT_BQ):
        tabs.append(np.where(k - q <= r * ATT_BQ, 0.0, NEG_BIG).astype(np.float32))
    return jnp.asarray(np.stack(tabs))


def _attn_kernel(q_ref, k_ref, vt_ref, bias_ref, o_ref, *scratch):
    s_scr = scratch[:ATT_SLOTS]
    p_scr = scratch[ATT_SLOTS:2 * ATT_SLOTS]
    acc_scr = scratch[2 * ATT_SLOTS]
    length = q_ref.shape[2]
    n_q = length // ATT_BQ
    n_items = _causal_items(length)
    assert n_items % ATT_SLOTS == 0

    def blocks_of(qi):
        return (qi * ATT_BQ) // ATT_BK + 1

    def advance(item):
        qi, j = item
        last = j + 1 == blocks_of(qi)
        return (jnp.minimum(jnp.where(last, qi + 1, qi), n_q - 1), jnp.where(last, 0, j + 1))

    def step(slot, item_s, item_x, item_v, chain, lane):
        m, acc = chain
        mb, alpha_q = lane
        out_chain, out_lane = [], []
        qi_v, j_v = item_v
        for hh in range(ATT_HEADS):
            vt = vt_ref[0, j_v, pl.ds(hh * VT_ROWS, VT_ROWS), :]
            acc_h = alpha_q[hh] * acc[hh] + _dot(vt, p_scr[slot][hh])
            acc_scr[qi_v, hh] = acc_h
            out_chain.append([None, acc_h])
        _, j_x = item_x
        for hh in range(ATT_HEADS):
            m_in = jnp.where(j_x == 0, NEG_BIG, m[hh])
            m_new = jnp.maximum(m_in, mb[hh])
            p_scr[slot][hh] = jnp.exp2(s_scr[slot][hh] - m_new).astype(BF16)
            out_chain[hh][0] = m_new
            out_lane.append([None, jnp.exp2(m_in - m_new)])
        qi_s, j_s = item_s
        shift = qi_s * ATT_BQ - j_s * ATT_BK
        sel = jnp.where(j_s + 1 == blocks_of(qi_s), 1 + shift // ATT_BQ, 0)
        for hh in range(ATT_HEADS):
            s = _dot_nt(k_ref[0, hh, pl.ds(j_s * ATT_BK, ATT_BK), :],
                        q_ref[0, hh, pl.ds(qi_s * ATT_BQ, ATT_BQ), :]) + bias_ref[sel]
            s_scr[slot][hh] = s
            out_lane[hh][0] = jnp.max(s, axis=0, keepdims=True)
        chain = tuple(tuple(c[i] for c in out_chain) for i in range(2))
        lane = tuple(tuple(c[i] for c in out_lane) for i in range(2))
        return chain, lane

    for slot in range(ATT_SLOTS):
        s_scr[slot][...] = jnp.full(s_scr[slot].shape, NULL_KEY, F32)
        p_scr[slot][...] = jnp.zeros(p_scr[slot].shape, BF16)

    def per_head(shape, v):
        pos = lax.broadcasted_iota(jnp.int32, shape, 0) + lax.broadcasted_iota(jnp.int32, shape, 1)
        return tuple(jnp.where(pos >= 0, v, 0.0).astype(F32) for _ in range(ATT_HEADS))

    row = (1, ATT_BQ)
    chain = (per_head(row, NEG_BIG), per_head((VT_ROWS, ATT_BQ), 0.0))
    lane = (per_head(row, NULL_KEY), per_head(row, 1.0))
    zero = jnp.int32(0)
    first = (zero, zero)

    def body(_, carry):
        item, hists, chain, lanes = carry
        new_hists, new_lanes = [], []
        for slot in range(ATT_SLOTS):
            item_x, item_v = hists[slot]
            chain, lane = step(slot, item, item_x, item_v, chain, lanes[slot])
            new_hists.append((item, item_x))
            new_lanes.append(lane)
            item = advance(item)
        return item, tuple(new_hists), chain, tuple(new_lanes)

    lax.fori_loop(0, n_items // ATT_SLOTS + 2, body,
                  (first, ((first, first),) * ATT_SLOTS, chain, (lane,) * ATT_SLOTS))

    def finish(qi, _):
        outs = []
        for hh in range(ATT_HEADS):
            a = acc_scr[qi, hh]
            outs.append(a[:HEAD_DIM] / a[HEAD_DIM:HEAD_DIM + 1])
        o_ref[0, pl.ds(qi * ATT_BQ, ATT_BQ), :] = jnp.concatenate(outs, axis=0).T.astype(BF16)
        return 0

    lax.fori_loop(0, n_q, finish, 0)


def _attention(q_aug, k_aug, vt):
    b, _, l, _ = q_aug.shape
    bias = _diag_bias()
    return pl.pallas_call(
        _attn_kernel,
        grid=(b, N_HEADS // ATT_HEADS),
        in_specs=[pl.BlockSpec((1, ATT_HEADS, l, HEAD_PAD), lambda i, h: (i, h, 0, 0)),
                  pl.BlockSpec((1, ATT_HEADS, l, HEAD_PAD), lambda i, h: (i, h, 0, 0)),
                  pl.BlockSpec((1, l // ATT_BK, ATT_HEADS * VT_ROWS, ATT_BK), lambda i, h: (i, 0, h, 0)),
                  pl.BlockSpec(bias.shape, lambda i, h: (0, 0, 0))],
        out_specs=pl.BlockSpec((1, l, ATT_HEADS * HEAD_DIM), lambda i, h: (i, 0, h)),
        out_shape=jax.ShapeDtypeStruct((b, l, D_MODEL), BF16),
        scratch_shapes=[pltpu.VMEM((ATT_HEADS, ATT_BK, ATT_BQ), F32)] * ATT_SLOTS
                       + [pltpu.VMEM((ATT_HEADS, ATT_BK, ATT_BQ), BF16)] * ATT_SLOTS
                       + [pltpu.VMEM((l // ATT_BQ, ATT_HEADS, VT_ROWS, ATT_BQ), F32)],
        compiler_params=_cparams(("parallel", "parallel")),
        name="fox_attention",
    )(q_aug, k_aug, vt, bias)


def _pad_heads(w):
    w3 = w.reshape(D_MODEL, N_HEADS, HEAD_DIM)
    w3 = jnp.pad(w3, ((0, 0), (0, 0), (0, HEAD_PAD - HEAD_DIM)))
    return w3.reshape(D_MODEL, N_HEADS * HEAD_PAD)


def kernel(x, mix_norm, mlp_norm, mlp_w1, mlp_w2, ssm_log_dt, ssm_a_re, ssm_a_im,
           ssm_b_re, ssm_b_im, ssm_c_re, ssm_c_im, ssm_d, ssm_w_glu, kv_norm, w_kvf, b_f,
           attn_wq, attn_wo, final_norm):
    bsz, length, _ = x.shape
    m = bsz * length
    nk = length // SSM_CHUNK
    h = x.reshape(m, D_MODEL)

    for i in range(DEPTH):
        if i < N_A_LAYERS:
            w2, pt, r, a1, a2 = _ssm_prep(ssm_log_dt[i], ssm_a_re[i], ssm_a_im[i], ssm_b_re[i],
                                          ssm_b_im[i], ssm_c_re[i], ssm_c_im[i])
            u = _norm_pack(h, mix_norm[i])
            v = _ssm_summary(u, r)
            s = _ssm_scan(v.reshape(bsz, nk, N_GROUPS, LANES),
                          a1.reshape(N_GROUPS, LANES), a2.reshape(N_GROUPS, LANES))
            d_t = jnp.tile(ssm_d[i].reshape(N_GROUPS, 1, SSM_GROUP), (1, 1, SSM_CHUNK))
            z = _ssm_output(u, s.reshape(bsz * nk, N_GROUPS * LANES), w2, pt, d_t)
            h = _glu(h, z, ssm_w_glu[i].astype(BF16))
        else:
            j = i - N_A_LAYERS
            wq_aug = _pad_heads(attn_wq[j] * (HEAD_DIM ** -0.5 * LOG2E)).astype(BF16)
            q_aug = _q_proj(h.reshape(bsz, length, D_MODEL), mix_norm[i], wq_aug, fcum)
            o = _attention(q_aug, k_aug, v_nat)
            h = _proj_res(h, o.reshape(m, D_MODEL), attn_wo[j].astype(BF16))
        h = _mlp(h, mlp_norm[i], mlp_w1[i].astype(BF16), mlp_w2[i].astype(BF16),
                 final_norm, final_norm=(i == DEPTH - 1))
        if i == N_A_LAYERS - 1:
            attn_dim = N_HEADS * HEAD_DIM
            wk_aug = _pad_heads(w_kvf[:, :attn_dim]).astype(BF16)
            wv = w_kvf[:, attn_dim:2 * attn_dim].T.reshape(N_HEADS, HEAD_DIM, D_MODEL)
            wv = jnp.pad(wv, ((0, 0), (0, VT_ROWS - HEAD_DIM), (0, 0)))
            wv = wv.reshape(N_HEADS * VT_ROWS, D_MODEL).astype(BF16)
            wf = jnp.pad(w_kvf[:, 2 * attn_dim:], ((0, 0), (0, LANES - N_HEADS))).astype(BF16)
            bf = jnp.pad(b_f, (0, LANES - N_HEADS)).reshape(1, LANES)
            k_aug, v_nat, fcum = _kv_proj(h.reshape(bsz, length, D_MODEL), kv_norm,
                                          wk_aug, wv, wf, bf)
    return h.reshape(bsz, length, D_MODEL)
```

```python
import functools
import math

import numpy as np
import jax
import jax.numpy as jnp
from jax import lax
from jax.experimental import pallas as pl
from jax.experimental.pallas import tpu as pltpu

D_MODEL = 1024
N_GROUPS = 64
SSM_GROUP = 16
GROUP_SHIFT = 4
SSM_STATE = 64
N_HEADS = 16
HEAD_DIM = 64
D_FF = 4 * D_MODEL
RMS_EPS = 1e-6
N_A_LAYERS = 2
DEPTH = 4

SSM_CHUNK = 64
CHUNK_W = SSM_CHUNK * SSM_GROUP
LANES = 128
HEAD_PAD = 128
N_EXTRA = 3

ROW_TILE = 512
FF_CHUNK = 1024
ATT_BQ = 256
VT_ROWS = 80
ATT_SLOTS = 8
ATT_HEADS = 2
ATT_BK = 512
NEG_BIG = -1e30
NULL_KEY = 3.0 * NEG_BIG
LOG2E = math.log2(math.e)

assert ROW_TILE == ATT_BK and ATT_BK % ATT_BQ == 0

VMEM_LIMIT = 56 * 1024 * 1024

F32 = jnp.float32
BF16 = jnp.bfloat16


def _cparams(sem):
    return pltpu.CompilerParams(dimension_semantics=sem, vmem_limit_bytes=VMEM_LIMIT)


def _rms(x, g):
    return x * lax.rsqrt(jnp.mean(x * x, axis=-1, keepdims=True) + RMS_EPS) * g


def _dot(a, b):
    return jnp.dot(a, b, preferred_element_type=F32)


def _dot_nt(a, b, precision=None):
    return lax.dot_general(a, b, (((1,), (1,)), ((), ())),
                           preferred_element_type=F32, precision=precision)


def _split3(x):
    hi = x.astype(BF16)
    r1 = x - hi.astype(F32)
    mid = r1.astype(BF16)
    lo = (r1 - mid.astype(F32)).astype(BF16)
    return hi, mid, lo


PACK_CHUNKS = 8
PIECES = LANES // SSM_GROUP
assert PACK_CHUNKS * SSM_CHUNK == ROW_TILE and PIECES == 8


def _piece_transpose(blocks):
    piece = lax.broadcasted_iota(jnp.int32, (PACK_CHUNKS, LANES), 1) >> GROUP_SHIFT
    for d in (4, 2, 1):
        keep = (piece & d) == 0
        nxt = []
        for v in blocks:
            out = list(v)
            for a in range(PIECES):
                if a & d:
                    continue
                b = a + d
                out[a] = jnp.where(keep, v[a], pltpu.roll(v[b], SSM_GROUP * d, axis=1))
                out[b] = jnp.where(keep, pltpu.roll(v[a], LANES - SSM_GROUP * d, axis=1), v[b])
            nxt.append(out)
        blocks = nxt
    return blocks


LANE_BLOCKS = D_MODEL // LANES


def _pack_chunks(tok_ref, grp_ref):
    n_sb = SSM_CHUNK // PIECES
    for gb in range(LANE_BLOCKS):
        v = [[tok_ref[gb, pl.ds(PIECES * sb + j, PACK_CHUNKS, stride=SSM_CHUNK), :]
              for j in range(PIECES)] for sb in range(n_sb)]
        w = _piece_transpose(v)
        for sb in range(n_sb):
            for i in range(PIECES):
                grp_ref[PIECES * gb + i, :, sb * LANES:(sb + 1) * LANES] = w[sb][i]


def _unpack_chunks(grp_ref, tok_ref, lane_blocks):
    n_sb = SSM_CHUNK // PIECES
    for gb in lane_blocks:
        w = [[grp_ref[PIECES * gb + i, :, sb * LANES:(sb + 1) * LANES] for i in range(PIECES)]
             for sb in range(n_sb)]
        v = _piece_transpose(w)
        for sb in range(n_sb):
            for j in range(PIECES):
                tok_ref[gb, pl.ds(PIECES * sb + j, PACK_CHUNKS, stride=SSM_CHUNK), :] = v[sb][j]


def _norm_pack_kernel(h_ref, g_ref, u_ref, xn_scr):
    xn = _rms(h_ref[...], g_ref[...])
    for gb in range(LANE_BLOCKS):
        xn_scr[gb] = xn[:, gb * LANES:(gb + 1) * LANES]
    _pack_chunks(xn_scr, u_ref)


def _norm_pack(h2, g):
    m = h2.shape[0]
    return pl.pallas_call(
        _norm_pack_kernel,
        grid=(m // ROW_TILE,),
        in_specs=[pl.BlockSpec((ROW_TILE, D_MODEL), lambda i: (i, 0)),
                  pl.BlockSpec((1, D_MODEL), lambda i: (0, 0))],
        out_specs=pl.BlockSpec((N_GROUPS, PACK_CHUNKS, CHUNK_W), lambda i: (0, i, 0)),
        out_shape=jax.ShapeDtypeStruct((N_GROUPS, m // SSM_CHUNK, CHUNK_W), F32),
        scratch_shapes=[pltpu.VMEM((LANE_BLOCKS, ROW_TILE, LANES), F32)],
        compiler_params=_cparams(("parallel",)),
        name="norm_pack",
    )(h2, g.reshape(1, D_MODEL))


def _swap_halves(x):
    return pltpu.roll(x, 64, axis=x.ndim - 1)


def _ssm_prep_kernel(ldt_ref, are_ref, aim_ref, bt_ref, cc_ref,
                     w2_ref, pt_ref, r_ref, a1_ref, a2_ref):
    lane1 = lax.broadcasted_iota(jnp.int32, (1, LANES), 1)
    lo1 = lane1 < SSM_STATE
    a_re = are_ref[0]
    a_im = aim_ref[0]
    dt = jnp.exp(ldt_ref[0])
    zr = a_re * dt
    zi = a_im * dt
    mag = jnp.exp(zr)
    lr = mag * jnp.cos(zi)
    li = mag * jnp.sin(zi)
    imag = jnp.exp(-zr)
    ir = imag * jnp.cos(zi)
    ii = -imag * jnp.sin(zi)

    sgn1 = jnp.where(lo1, -1.0, 1.0)

    def powers(count, nbits, wr, wi, descending=False):
        n = lax.broadcasted_iota(jnp.int32, (count, 1, LANES), 0)
        if descending:
            n = count - 1 - n
        lo3 = lax.broadcasted_iota(jnp.int32, (count, 1, LANES), 2) < SSM_STATE
        p = jnp.where(lo3, 1.0, 0.0).astype(F32)
        ps = jnp.where(lo3, 0.0, 1.0).astype(F32)
        for k in range(nbits):
            bit = ((n >> k) & 1) == 1
            wn = (sgn1 * wi)[None]
            p, ps = (jnp.where(bit, p * wr[None] + ps * wn, p),
                     jnp.where(bit, ps * wr[None] - p * wn, ps))
            wr, wi = wr * wr - wi * wi, 2.0 * wr * wi
        return p, ps

    def times_coef(table, coef, rows):
        p, ps = table
        cs = _swap_halves(coef)
        lo = lax.broadcasted_iota(jnp.int32, coef.shape, 1) < SSM_STATE
        prod = p * jnp.where(lo, coef, cs)[None] + ps * jnp.where(lo, -cs, coef)[None]
        return prod.reshape(rows, LANES)

    nr = lr - 1.0
    den = a_re * a_re + a_im * a_im
    cr = (nr * a_re + li * a_im) / den
    ci = (li * a_re - nr * a_im) / den
    bt = bt_ref[0]
    bbar = bt * cr + _swap_halves(bt) * (sgn1 * ci)
    cc = cc_ref[0]

    lo_big = lax.broadcasted_iota(jnp.int32, (CHUNK_W, LANES), 1) < SSM_STATE
    p_pos = powers(SSM_CHUNK, 6, lr, li)
    cbig = times_coef(p_pos, cc, CHUNK_W)
    wn = (sgn1 * li)[None]
    p_next = (p_pos[0] * lr[None] + p_pos[1] * wn, p_pos[1] * lr[None] - p_pos[0] * wn)
    cl1 = times_coef(p_next, cc, CHUNK_W)
    pt_ref[0] = jnp.where(lo_big, cl1, -cl1).astype(BF16)
    r_ref[0] = times_coef(powers(SSM_CHUNK, 6, lr, li, descending=True), bbar, CHUNK_W).astype(BF16)

    bsmall = times_coef(powers(LANES // SSM_GROUP, 3, ir, ii), bbar, LANES)
    lo8 = lax.broadcasted_iota(jnp.int32, (LANES, LANES), 1) < SSM_STATE
    lhs = jnp.where(lo8, bsmall, -bsmall)
    w2 = _dot_nt(lhs, cbig, precision=lax.Precision.HIGHEST)
    n8 = lax.broadcasted_iota(jnp.int32, (LANES, 1), 0) >> GROUP_SHIFT
    col_t = lax.broadcasted_iota(jnp.int32, (LANES, CHUNK_W), 1) >> GROUP_SHIFT
    w2_ref[0] = jnp.where(col_t >= n8, w2, 0.0).astype(BF16)

    ar, ai = lr, li
    for _ in range(6):
        ar, ai = ar * ar - ai * ai, 2.0 * ar * ai
    a1_ref[0] = ar
    a2_ref[0] = jnp.where(lo1, -ai, ai)


def _ssm_prep(log_dt, a_re, a_im, b_re, b_im, c_re, c_im):
    g = N_GROUPS
    dup = lambda x: jnp.concatenate([x, x], axis=-1).reshape(g, 1, LANES)
    ldt = jnp.broadcast_to(log_dt.reshape(g, 1, 1), (g, 1, LANES))
    bt = jnp.concatenate([b_re.transpose(0, 2, 1), b_im.transpose(0, 2, 1)], axis=-1)
    cc = jnp.concatenate([c_re, c_im], axis=-1)
    vec = pl.BlockSpec((1, 1, LANES), lambda i: (i, 0, 0))
    mat = pl.BlockSpec((1, SSM_GROUP, LANES), lambda i: (i, 0, 0))
    return pl.pallas_call(
        _ssm_prep_kernel,
        grid=(g,),
        in_specs=[vec, vec, vec, mat, mat],
        out_specs=[pl.BlockSpec((1, LANES, CHUNK_W), lambda i: (i, 0, 0)),
                   pl.BlockSpec((1, CHUNK_W, LANES), lambda i: (i, 0, 0)),
                   pl.BlockSpec((1, CHUNK_W, LANES), lambda i: (i, 0, 0)),
                   vec, vec],
        out_shape=[jax.ShapeDtypeStruct((g, LANES, CHUNK_W), BF16),
                   jax.ShapeDtypeStruct((g, CHUNK_W, LANES), BF16),
                   jax.ShapeDtypeStruct((g, CHUNK_W, LANES), BF16),
                   jax.ShapeDtypeStruct((g, 1, LANES), F32),
                   jax.ShapeDtypeStruct((g, 1, LANES), F32)],
        compiler_params=_cparams(("parallel",)),
        name="ssm_prep",
    )(ldt, dup(a_re), dup(a_im), bt, cc)


def _ssm_summary_kernel(u_ref, r_ref, v_ref):
    v_ref[...] = _dot(u_ref[0].astype(BF16), r_ref[0])


def _ssm_summary(u, r):
    g, m, _ = u.shape
    return pl.pallas_call(
        _ssm_summary_kernel,
        grid=(g,),
        in_specs=[pl.BlockSpec((1, m, CHUNK_W), lambda i: (i, 0, 0)),
                  pl.BlockSpec((1, CHUNK_W, LANES), lambda i: (i, 0, 0))],
        out_specs=pl.BlockSpec((m, LANES), lambda i: (0, i)),
        out_shape=jax.ShapeDtypeStruct((m, g * LANES), F32),
        compiler_params=_cparams(("parallel",)),
        name="ssm_summary",
    )(u, r)


def _ssm_scan_kernel(v_ref, a1_ref, a2_ref, s_ref):
    a1 = a1_ref[...]
    a2 = a2_ref[...]
    nk = v_ref.shape[1]

    def step(k, s):
        s_ref[0, k] = s
        return s * a1 + _swap_halves(s) * a2 + v_ref[0, k]

    lax.fori_loop(0, nk, step, jnp.zeros((N_GROUPS, LANES), F32))


def _ssm_scan(v4, a1, a2):
    b, nk = v4.shape[0], v4.shape[1]
    blk = pl.BlockSpec((1, nk, N_GROUPS, LANES), lambda i: (i, 0, 0, 0))
    coef = pl.BlockSpec((N_GROUPS, LANES), lambda i: (0, 0))
    return pl.pallas_call(
        _ssm_scan_kernel,
        grid=(b,),
        in_specs=[blk, coef, coef],
        out_specs=blk,
        out_shape=jax.ShapeDtypeStruct(v4.shape, F32),
        compiler_params=_cparams(("parallel",)),
        name="ssm_scan",
    )(v4, a1, a2)


def _ssm_output_kernel(u_ref, s_ref, w2_ref, pt_ref, d_ref, z_ref, toep_ref):
    @pl.when(pl.program_id(0) == 0)
    def _():
        toep_ref[...] = jnp.zeros_like(toep_ref)

    nblk = CHUNK_W // LANES
    for i in range(nblk):
        toep_ref[i * LANES:(i + 1) * LANES, i * LANES:] = w2_ref[0, :, :CHUNK_W - i * LANES]

    u = u_ref[0]
    ub = u.astype(BF16)
    y = _dot_nt(s_ref[...].astype(BF16), pt_ref[0]) + d_ref[0] * u
    tile = 256
    cols = []
    for j in range(CHUNK_W // tile):
        kk = (j + 1) * tile
        cols.append(_dot(ub[:, :kk], toep_ref[:kk, j * tile:(j + 1) * tile]))
    y = y + jnp.concatenate(cols, axis=1)
    c0 = math.sqrt(2.0 / math.pi)
    z_ref[0] = 0.5 * y * (1.0 + jnp.tanh(c0 * (y + 0.044715 * (y * y * y))))


def _ssm_output(u, s, w2, pt, d_t):
    g, m, _ = u.shape
    return pl.pallas_call(
        _ssm_output_kernel,
        grid=(g,),
        in_specs=[pl.BlockSpec((1, m, CHUNK_W), lambda i: (i, 0, 0)),
                  pl.BlockSpec((m, LANES), lambda i: (0, i)),
                  pl.BlockSpec((1, LANES, CHUNK_W), lambda i: (i, 0, 0)),
                  pl.BlockSpec((1, CHUNK_W, LANES), lambda i: (i, 0, 0)),
                  pl.BlockSpec((1, 1, CHUNK_W), lambda i: (i, 0, 0))],
        out_specs=pl.BlockSpec((1, m, CHUNK_W), lambda i: (i, 0, 0)),
        out_shape=jax.ShapeDtypeStruct((g, m, CHUNK_W), F32),
        scratch_shapes=[pltpu.VMEM((CHUNK_W, CHUNK_W), BF16)],
        compiler_params=_cparams(("arbitrary",)),
        name="ssm_output",
    )(u, s, w2, pt, d_t)


def _glu_kernel(h_ref, z_ref, w_ref, o_ref, z_scr):
    zw = None
    for first in range(0, LANE_BLOCKS, 2):
        _unpack_chunks(z_ref, z_scr, (first, first + 1))
        z = jnp.concatenate([z_scr[first].astype(BF16), z_scr[first + 1].astype(BF16)], axis=1)
        part = _dot(z, w_ref[first * LANES:(first + 2) * LANES, :])
        zw = part if zw is None else zw + part
    o_ref[...] = h_ref[...] + zw[:, :D_MODEL] * jax.nn.sigmoid(zw[:, D_MODEL:])


def _glu(h2, z, w):
    m = h2.shape[0]
    return pl.pallas_call(
        _glu_kernel,
        grid=(m // ROW_TILE,),
        in_specs=[pl.BlockSpec((ROW_TILE, D_MODEL), lambda i: (i, 0)),
                  pl.BlockSpec((N_GROUPS, PACK_CHUNKS, CHUNK_W), lambda i: (0, i, 0)),
                  pl.BlockSpec(w.shape, lambda i: (0, 0))],
        out_specs=pl.BlockSpec((ROW_TILE, D_MODEL), lambda i: (i, 0)),
        out_shape=jax.ShapeDtypeStruct((m, D_MODEL), F32),
        scratch_shapes=[pltpu.VMEM((LANE_BLOCKS, ROW_TILE, LANES), F32)],
        compiler_params=_cparams(("parallel",)),
        name="glu",
    )(h2, z, w)


def _mlp_kernel(h_ref, g_ref, w1_ref, w2_ref, fg_ref, *rest, final_norm, with_proj):
    if with_proj:
        a_ref, wo_ref, o_ref = rest
        x = h_ref[...] + _dot(a_ref[...], wo_ref[...])
    else:
        (o_ref,) = rest
        x = h_ref[...]
    xn = _rms(x, g_ref[...]).astype(BF16)
    acc = x
    for f in range(0, D_FF, FF_CHUNK):
        a = jnp.square(jnp.maximum(_dot(xn, w1_ref[:, f:f + FF_CHUNK]), 0.0))
        acc = acc + _dot(a.astype(BF16), w2_ref[f:f + FF_CHUNK, :])
    if final_norm:
        acc = _rms(acc, fg_ref[...])
    o_ref[...] = acc


def _mlp(h2, g, w1, w2, fg, final_norm, proj=None):
    m = h2.shape[0]
    once = dict(pipeline_mode=pl.Buffered(1))
    in_specs = [pl.BlockSpec((ROW_TILE, D_MODEL), lambda i: (i, 0)),
                pl.BlockSpec((1, D_MODEL), lambda i: (0, 0)),
                pl.BlockSpec((D_MODEL, D_FF), lambda i: (0, 0), **once),
                pl.BlockSpec((D_FF, D_MODEL), lambda i: (0, 0), **once),
                pl.BlockSpec((1, D_MODEL), lambda i: (0, 0))]
    args = [h2, g.reshape(1, D_MODEL), w1, w2, fg.reshape(1, D_MODEL)]
    if proj is not None:
        a, wo = proj
        in_specs += [pl.BlockSpec((ROW_TILE, a.shape[1]), lambda i: (i, 0)),
                     pl.BlockSpec(wo.shape, lambda i: (0, 0), **once)]
        args += [a, wo]
    return pl.pallas_call(
        functools.partial(_mlp_kernel, final_norm=final_norm, with_proj=proj is not None),
        grid=(m // ROW_TILE,),
        in_specs=in_specs,
        out_specs=pl.BlockSpec((ROW_TILE, D_MODEL), lambda i: (i, 0)),
        out_shape=jax.ShapeDtypeStruct((m, D_MODEL), F32),
        compiler_params=_cparams(("parallel",)),
        name="mlp_final" if final_norm else "mlp",
    )(*args)


def _extra_scatter(sign_f, f_first):
    scat = np.zeros((LANES, N_HEADS * HEAD_PAD), np.float32)
    const = np.zeros((1, N_HEADS * HEAD_PAD), np.float32)
    for h in range(N_HEADS):
        base = h * HEAD_PAD + HEAD_DIM
        f0, o0 = (0, N_EXTRA) if f_first else (N_EXTRA, 0)
        for j in range(N_EXTRA):
            scat[j * N_HEADS + h, base + f0 + j] = sign_f
            const[0, base + o0 + j] = 1.0
    return jnp.asarray(scat, BF16), jnp.asarray(const, F32)


def _augment(proj, f_log2, sc_ref, cst_ref, out_ref):
    hi = f_log2.astype(BF16).astype(F32)
    r1 = f_log2 - hi
    mid = r1.astype(BF16).astype(F32)
    lo = (r1 - mid).astype(BF16).astype(F32)
    terms = hi + pltpu.roll(mid, N_HEADS, axis=1) + pltpu.roll(lo, 2 * N_HEADS, axis=1)
    extra = _dot(terms.astype(BF16), sc_ref[...]) + cst_ref[...]
    lane = lax.broadcasted_iota(jnp.int32, (proj.shape[0], LANES), 1)
    for hh in range(N_HEADS):
        blk = proj[:, (hh // 2) * LANES:(hh // 2 + 1) * LANES]
        if hh % 2:
            blk = pltpu.roll(blk, HEAD_DIM, axis=1)
        out_ref[0, hh] = jnp.where(lane < HEAD_DIM, blk,
                                   extra[:, hh * HEAD_PAD:(hh + 1) * HEAD_PAD]).astype(BF16)


def _kv_kernel(h_ref, g_ref, wk_ref, wv_ref, vone_ref, wf_ref, bf_ref, sc_ref, cst_ref,
               k_ref, v_ref, f_ref, carry_ref):
    @pl.when(pl.program_id(1) == 0)
    def _():
        carry_ref[...] = jnp.zeros_like(carry_ref)

    xn = _rms(h_ref[0], g_ref[...]).astype(BF16)
    v_ref[0, 0] = (_dot_nt(wv_ref[...], xn) + vone_ref[...]).astype(BF16)

    logit = _dot(xn, wf_ref[...]) + bf_ref[...]
    log_f = jnp.minimum(logit, 0.0) - jnp.log1p(jnp.exp(-jnp.abs(logit)))
    lane = lax.broadcasted_iota(jnp.int32, log_f.shape, 1)
    log_f = jnp.where(lane < N_HEADS, log_f, 0.0)
    t = log_f.shape[0]
    tri = (lax.broadcasted_iota(jnp.int32, (t, t), 0)
           >= lax.broadcasted_iota(jnp.int32, (t, t), 1)).astype(BF16)
    hi, mid, lo = _split3(log_f)
    cum = (_dot(tri, hi) + _dot(tri, mid)) + _dot(tri, lo) + carry_ref[...]
    carry_ref[...] = cum[t - 1:t, :]
    f_ref[0] = cum

    _augment(_dot(xn, wk_ref[...]), cum * LOG2E, sc_ref, cst_ref, k_ref)


def _kv_proj(h3, g, wk, wv, wf, bf):
    b, l, _ = h3.shape
    nt = l // ROW_TILE
    scat, const = _extra_scatter(-1.0, f_first=False)
    vone = np.zeros((N_HEADS, VT_ROWS, 1), np.float32)
    vone[:, HEAD_DIM] = 1.0
    vone = jnp.asarray(vone.reshape(N_HEADS * VT_ROWS, 1))
    full = lambda a: pl.BlockSpec(a.shape, lambda i, j: (0,) * a.ndim)
    return pl.pallas_call(
        _kv_kernel,
        grid=(b, nt),
        in_specs=[pl.BlockSpec((1, ROW_TILE, D_MODEL), lambda i, j: (i, j, 0)),
                  pl.BlockSpec((1, D_MODEL), lambda i, j: (0, 0)),
                  full(wk), full(wv), full(vone), full(wf), full(bf), full(scat), full(const)],
        out_specs=[pl.BlockSpec((1, N_HEADS, ROW_TILE, HEAD_PAD), lambda i, j: (i, 0, j, 0)),
                   pl.BlockSpec((1, 1, N_HEADS * VT_ROWS, ROW_TILE), lambda i, j: (i, j, 0, 0)),
                   pl.BlockSpec((1, ROW_TILE, LANES), lambda i, j: (i, j, 0))],
        out_shape=[jax.ShapeDtypeStruct((b, N_HEADS, l, HEAD_PAD), BF16),
                   jax.ShapeDtypeStruct((b, nt, N_HEADS * VT_ROWS, ROW_TILE), BF16),
                   jax.ShapeDtypeStruct((b, l, LANES), F32)],
        scratch_shapes=[pltpu.VMEM((1, LANES), F32)],
        compiler_params=_cparams(("parallel", "arbitrary")),
        name="kv_proj",
    )(h3, g.reshape(1, D_MODEL), wk, wv, vone, wf, bf, scat, const)


def _q_kernel(h_ref, g_ref, wq_ref, f_ref, sc_ref, cst_ref, q_ref):
    xn = _rms(h_ref[0], g_ref[...]).astype(BF16)
    _augment(_dot(xn, wq_ref[...]), f_ref[0] * LOG2E, sc_ref, cst_ref, q_ref)


def _q_proj(h3, g, wq, fcum):
    b, l, _ = h3.shape
    scat, const = _extra_scatter(1.0, f_first=True)
    full = lambda a: pl.BlockSpec(a.shape, lambda i, j: (0,) * a.ndim)
    return pl.pallas_call(
        _q_kernel,
        grid=(b, l // ROW_TILE),
        in_specs=[pl.BlockSpec((1, ROW_TILE, D_MODEL), lambda i, j: (i, j, 0)),
                  pl.BlockSpec((1, D_MODEL), lambda i, j: (0, 0)),
                  full(wq),
                  pl.BlockSpec((1, ROW_TILE, LANES), lambda i, j: (i, j, 0)),
                  full(scat), full(const)],
        out_specs=pl.BlockSpec((1, N_HEADS, ROW_TILE, HEAD_PAD), lambda i, j: (i, 0, j, 0)),
        out_shape=jax.ShapeDtypeStruct((b, N_HEADS, l, HEAD_PAD), BF16),
        compiler_params=_cparams(("parallel", "parallel")),
        name="q_proj",
    )(h3, g.reshape(1, D_MODEL), wq, fcum, scat, const)


def _causal_items(length):
    return sum((qi * ATT_BQ) // ATT_BK + 1 for qi in range(length // ATT_BQ))


def _diag_bias():
    k = np.arange(ATT_BK)[:, None]
    q = np.arange(ATT_BQ)[None, :]
    tabs = [np.zeros((ATT_BK, ATT_BQ), np.float32)]
    for r in range(ATT_BK // ATT_BQ):
        tabs.append(np.where(k - q <= r * ATT_BQ, 0.0, NEG_BIG).astype(np.float32))
    return jnp.asarray(np.stack(tabs))


def _attn_kernel(q_ref, k_ref, vt_ref, bias_ref, o_ref, *scratch):
    s_scr = scratch[:ATT_SLOTS]
    p_scr = scratch[ATT_SLOTS:2 * ATT_SLOTS]
    acc_scr = scratch[2 * ATT_SLOTS]
    length = q_ref.shape[2]
    n_q = length // ATT_BQ
    n_items = _causal_items(length)
    assert n_items % ATT_SLOTS == 0

    def blocks_of(qi):
        return (qi * ATT_BQ) // ATT_BK + 1

    def advance(item):
        qi, j = item
        last = j + 1 == blocks_of(qi)
        return (jnp.minimum(jnp.where(last, qi + 1, qi), n_q - 1), jnp.where(last, 0, j + 1))

    def step(slot, item_s, item_x, item_v, chain, lane):
        m, acc = chain
        mb, alpha_q = lane
        out_chain, out_lane = [], []
        qi_v, j_v = item_v
        for hh in range(ATT_HEADS):
            vt = vt_ref[0, j_v, pl.ds(hh * VT_ROWS, VT_ROWS), :]
            acc_h = alpha_q[hh] * acc[hh] + _dot(vt, p_scr[slot][hh])
            acc_scr[qi_v, hh] = acc_h
            out_chain.append([None, acc_h])
        _, j_x = item_x
        for hh in range(ATT_HEADS):
            m_in = jnp.where(j_x == 0, NEG_BIG, m[hh])
            m_new = jnp.maximum(m_in, mb[hh])
            p_scr[slot][hh] = jnp.exp2(s_scr[slot][hh] - m_new).astype(BF16)
            out_chain[hh][0] = m_new
            out_lane.append([None, jnp.exp2(m_in - m_new)])
        qi_s, j_s = item_s
        shift = qi_s * ATT_BQ - j_s * ATT_BK
        sel = jnp.where(j_s + 1 == blocks_of(qi_s), 1 + shift // ATT_BQ, 0)
        for hh in range(ATT_HEADS):
            s = _dot_nt(k_ref[0, hh, pl.ds(j_s * ATT_BK, ATT_BK), :],
                        q_ref[0, hh, pl.ds(qi_s * ATT_BQ, ATT_BQ), :]) + bias_ref[sel]
            s_scr[slot][hh] = s
            out_lane[hh][0] = jnp.max(s, axis=0, keepdims=True)
        chain = tuple(tuple(c[i] for c in out_chain) for i in range(2))
        lane = tuple(tuple(c[i] for c in out_lane) for i in range(2))
        return chain, lane

    for slot in range(ATT_SLOTS):
        s_scr[slot][...] = jnp.full(s_scr[slot].shape, NULL_KEY, F32)
        p_scr[slot][...] = jnp.zeros(p_scr[slot].shape, BF16)

    def per_head(shape, v):
        pos = lax.broadcasted_iota(jnp.int32, shape, 0) + lax.broadcasted_iota(jnp.int32, shape, 1)
        return tuple(jnp.where(pos >= 0, v, 0.0).astype(F32) for _ in range(ATT_HEADS))

    row = (1, ATT_BQ)
    chain = (per_head(row, NEG_BIG), per_head((VT_ROWS, ATT_BQ), 0.0))
    lane = (per_head(row, NULL_KEY), per_head(row, 1.0))
    zero = jnp.int32(0)
    first = (zero, zero)

    def body(_, carry):
        item, hists, chain, lanes = carry
        new_hists, new_lanes = [], []
        for slot in range(ATT_SLOTS):
            item_x, item_v = hists[slot]
            chain, lane = step(slot, item, item_x, item_v, chain, lanes[slot])
            new_hists.append((item, item_x))
            new_lanes.append(lane)
            item = advance(item)
        return item, tuple(new_hists), chain, tuple(new_lanes)

    lax.fori_loop(0, n_items // ATT_SLOTS + 2, body,
                  (first, ((first, first),) * ATT_SLOTS, chain, (lane,) * ATT_SLOTS))

    def finish(qi, _):
        outs = []
        for hh in range(ATT_HEADS):
            a = acc_scr[qi, hh]
            outs.append(a[:HEAD_DIM] / a[HEAD_DIM:HEAD_DIM + 1])
        o_ref[0, pl.ds(qi * ATT_BQ, ATT_BQ), :] = jnp.concatenate(outs, axis=0).T.astype(BF16)
        return 0

    lax.fori_loop(0, n_q, finish, 0)


def _attention(q_aug, k_aug, vt):
    b, _, l, _ = q_aug.shape
    bias = _diag_bias()
    return pl.pallas_call(
        _attn_kernel,
        grid=(b, N_HEADS // ATT_HEADS),
        in_specs=[pl.BlockSpec((1, ATT_HEADS, l, HEAD_PAD), lambda i, h: (i, h, 0, 0)),
                  pl.BlockSpec((1, ATT_HEADS, l, HEAD_PAD), lambda i, h: (i, h, 0, 0)),
                  pl.BlockSpec((1, l // ATT_BK, ATT_HEADS * VT_ROWS, ATT_BK), lambda i, h: (i, 0, h, 0)),
                  pl.BlockSpec(bias.shape, lambda i, h: (0, 0, 0))],
        out_specs=pl.BlockSpec((1, l, ATT_HEADS * HEAD_DIM), lambda i, h: (i, 0, h)),
        out_shape=jax.ShapeDtypeStruct((b, l, D_MODEL), BF16),
        scratch_shapes=[pltpu.VMEM((ATT_HEADS, ATT_BK, ATT_BQ), F32)] * ATT_SLOTS
                       + [pltpu.VMEM((ATT_HEADS, ATT_BK, ATT_BQ), BF16)] * ATT_SLOTS
                       + [pltpu.VMEM((l // ATT_BQ, ATT_HEADS, VT_ROWS, ATT_BQ), F32)],
        compiler_params=_cparams(("parallel", "parallel")),
        name="fox_attention",
    )(q_aug, k_aug, vt, bias)


def kernel(x, mix_norm, mlp_norm, mlp_w1, mlp_w2, ssm_log_dt, ssm_a_re, ssm_a_im,
           ssm_b_re, ssm_b_im, ssm_c_re, ssm_c_im, ssm_d, ssm_w_glu, kv_norm, w_kvf, b_f,
           attn_wq, attn_wo, final_norm):
    bsz, length, _ = x.shape
    m = bsz * length
    nk = length // SSM_CHUNK
    h = x.reshape(m, D_MODEL)

    for i in range(DEPTH):
        if i < N_A_LAYERS:
            w2, pt, r, a1, a2 = _ssm_prep(ssm_log_dt[i], ssm_a_re[i], ssm_a_im[i], ssm_b_re[i],
                                          ssm_b_im[i], ssm_c_re[i], ssm_c_im[i])
            u = _norm_pack(h, mix_norm[i])
            v = _ssm_summary(u, r)
            s = _ssm_scan(v.reshape(bsz, nk, N_GROUPS, LANES),
                          a1.reshape(N_GROUPS, LANES), a2.reshape(N_GROUPS, LANES))
            d_t = jnp.tile(ssm_d[i].reshape(N_GROUPS, 1, SSM_GROUP), (1, 1, SSM_CHUNK))
            z = _ssm_output(u, s.reshape(bsz * nk, N_GROUPS * LANES), w2, pt, d_t)
            h = _glu(h, z, ssm_w_glu[i].astype(BF16))
            proj = None
        else:
            j = i - N_A_LAYERS
            wq = (attn_wq[j] * (HEAD_DIM ** -0.5 * LOG2E)).astype(BF16)
            q_aug = _q_proj(h.reshape(bsz, length, D_MODEL), mix_norm[i], wq, fcum)
            o = _attention(q_aug, k_aug, v_nat)
            proj = (o.reshape(m, D_MODEL), attn_wo[j].astype(BF16))
        h = _mlp(h, mlp_norm[i], mlp_w1[i].astype(BF16), mlp_w2[i].astype(BF16),
                 final_norm, final_norm=(i == DEPTH - 1), proj=proj)
        if i == N_A_LAYERS - 1:
            attn_dim = N_HEADS * HEAD_DIM
            wk = w_kvf[:, :attn_dim].astype(BF16)
            wv = w_kvf[:, attn_dim:2 * attn_dim].T.reshape(N_HEADS, HEAD_DIM, D_MODEL)
            wv = jnp.pad(wv, ((0, 0), (0, VT_ROWS - HEAD_DIM), (0, 0)))
            wv = wv.reshape(N_HEADS * VT_ROWS, D_MODEL).astype(BF16)
            wf = jnp.pad(w_kvf[:, 2 * attn_dim:], ((0, 0), (0, LANES - N_HEADS))).astype(BF16)
            bf = jnp.pad(b_f, (0, LANES - N_HEADS)).reshape(1, LANES)
            k_aug, v_nat, fcum = _kv_proj(h.reshape(bsz, length, D_MODEL), kv_norm,
                                          wk, wv, wf, bf)
    return h.reshape(bsz, length, D_MODEL)
```

```python
import functools
import math

import numpy as np
import jax
import jax.numpy as jnp
from jax import lax
from jax.experimental import pallas as pl
from jax.experimental.pallas import tpu as pltpu

D_MODEL = 1024
N_GROUPS = 64
SSM_GROUP = 16
GROUP_SHIFT = 4
SSM_STATE = 64
N_HEADS = 16
HEAD_DIM = 64
D_FF = 4 * D_MODEL
RMS_EPS = 1e-6
N_A_LAYERS = 2
DEPTH = 4

SSM_CHUNK = 64
CHUNK_W = SSM_CHUNK * SSM_GROUP
LANES = 128
HEAD_PAD = 128
N_EXTRA = 3

ROW_TILE = 512
FF_CHUNK = 1024
ATT_BQ = 256
VT_ROWS = 80
ATT_SLOTS = 8
SSM_GROUPS_PER_STEP = 4
GLU_SLAB = 2
FINISH_UNROLL = 4
ATT_HEADS = 2
ATT_BK = 512
NEG_BIG = -1e30
NULL_KEY = 3.0 * NEG_BIG
LOG2E = math.log2(math.e)

assert ROW_TILE == ATT_BK and ATT_BK % ATT_BQ == 0

VMEM_LIMIT = 56 * 1024 * 1024

F32 = jnp.float32
BF16 = jnp.bfloat16


def _cparams(sem):
    return pltpu.CompilerParams(dimension_semantics=sem, vmem_limit_bytes=VMEM_LIMIT)


def _rms(x, g):
    return x * lax.rsqrt(jnp.mean(x * x, axis=-1, keepdims=True) + RMS_EPS) * g


def _dot(a, b):
    return jnp.dot(a, b, preferred_element_type=F32)


def _dot_nt(a, b, precision=None):
    return lax.dot_general(a, b, (((1,), (1,)), ((), ())),
                           preferred_element_type=F32, precision=precision)


def _split3(x):
    hi = x.astype(BF16)
    r1 = x - hi.astype(F32)
    mid = r1.astype(BF16)
    lo = (r1 - mid.astype(F32)).astype(BF16)
    return hi, mid, lo


PACK_CHUNKS = 8
PIECES = LANES // SSM_GROUP
assert PACK_CHUNKS * SSM_CHUNK == ROW_TILE and PIECES == 8


def _piece_transpose(blocks):
    piece = lax.broadcasted_iota(jnp.int32, (PACK_CHUNKS, LANES), 1) >> GROUP_SHIFT
    for d in (4, 2, 1):
        keep = (piece & d) == 0
        nxt = []
        for v in blocks:
            out = list(v)
            for a in range(PIECES):
                if a & d:
                    continue
                b = a + d
                out[a] = jnp.where(keep, v[a], pltpu.roll(v[b], SSM_GROUP * d, axis=1))
                out[b] = jnp.where(keep, pltpu.roll(v[a], LANES - SSM_GROUP * d, axis=1), v[b])
            nxt.append(out)
        blocks = nxt
    return blocks


LANE_BLOCKS = D_MODEL // LANES


def _pack_chunks(tok_ref, grp_ref):
    n_sb = SSM_CHUNK // PIECES
    for gb in range(LANE_BLOCKS):
        v = [[tok_ref[gb, pl.ds(PIECES * sb + j, PACK_CHUNKS, stride=SSM_CHUNK), :]
              for j in range(PIECES)] for sb in range(n_sb)]
        w = _piece_transpose(v)
        for sb in range(n_sb):
            for i in range(PIECES):
                grp_ref[PIECES * gb + i, :, sb * LANES:(sb + 1) * LANES] = w[sb][i]


def _unpack_chunks(grp_ref, tok_ref, lane_blocks):
    n_sb = SSM_CHUNK // PIECES
    for gb in lane_blocks:
        w = [[grp_ref[PIECES * gb + i, :, sb * LANES:(sb + 1) * LANES] for i in range(PIECES)]
             for sb in range(n_sb)]
        v = _piece_transpose(w)
        for sb in range(n_sb):
            for j in range(PIECES):
                tok_ref[gb, pl.ds(PIECES * sb + j, PACK_CHUNKS, stride=SSM_CHUNK), :] = v[sb][j]


def _norm_pack_kernel(h_ref, g_ref, u_ref, xn_scr):
    xn = _rms(h_ref[...], g_ref[...])
    for gb in range(LANE_BLOCKS):
        xn_scr[gb] = xn[:, gb * LANES:(gb + 1) * LANES]
    _pack_chunks(xn_scr, u_ref)


def _norm_pack(h2, g):
    m = h2.shape[0]
    return pl.pallas_call(
        _norm_pack_kernel,
        grid=(m // ROW_TILE,),
        in_specs=[pl.BlockSpec((ROW_TILE, D_MODEL), lambda i: (i, 0)),
                  pl.BlockSpec((1, D_MODEL), lambda i: (0, 0))],
        out_specs=pl.BlockSpec((N_GROUPS, PACK_CHUNKS, CHUNK_W), lambda i: (0, i, 0)),
        out_shape=jax.ShapeDtypeStruct((N_GROUPS, m // SSM_CHUNK, CHUNK_W), F32),
        scratch_shapes=[pltpu.VMEM((LANE_BLOCKS, ROW_TILE, LANES), F32)],
        compiler_params=_cparams(("parallel",)),
        name="norm_pack",
    )(h2, g.reshape(1, D_MODEL))


def _swap_halves(x):
    return pltpu.roll(x, 64, axis=x.ndim - 1)


def _ssm_prep_kernel(ldt_ref, are_ref, aim_ref, bt_ref, cc_ref,
                     w2_ref, pt_ref, r_ref, a1_ref, a2_ref):
    lane1 = lax.broadcasted_iota(jnp.int32, (1, LANES), 1)
    lo1 = lane1 < SSM_STATE
    a_re = are_ref[0]
    a_im = aim_ref[0]
    dt = jnp.exp(ldt_ref[0])
    zr = a_re * dt
    zi = a_im * dt
    mag = jnp.exp(zr)
    lr = mag * jnp.cos(zi)
    li = mag * jnp.sin(zi)
    imag = jnp.exp(-zr)
    ir = imag * jnp.cos(zi)
    ii = -imag * jnp.sin(zi)

    sgn1 = jnp.where(lo1, -1.0, 1.0)

    def powers(count, nbits, wr, wi, descending=False):
        n = lax.broadcasted_iota(jnp.int32, (count, 1, LANES), 0)
        if descending:
            n = count - 1 - n
        lo3 = lax.broadcasted_iota(jnp.int32, (count, 1, LANES), 2) < SSM_STATE
        p = jnp.where(lo3, 1.0, 0.0).astype(F32)
        ps = jnp.where(lo3, 0.0, 1.0).astype(F32)
        for k in range(nbits):
            bit = ((n >> k) & 1) == 1
            wn = (sgn1 * wi)[None]
            p, ps = (jnp.where(bit, p * wr[None] + ps * wn, p),
                     jnp.where(bit, ps * wr[None] - p * wn, ps))
            wr, wi = wr * wr - wi * wi, 2.0 * wr * wi
        return p, ps

    def times_coef(table, coef, rows):
        p, ps = table
        cs = _swap_halves(coef)
        lo = lax.broadcasted_iota(jnp.int32, coef.shape, 1) < SSM_STATE
        prod = p * jnp.where(lo, coef, cs)[None] + ps * jnp.where(lo, -cs, coef)[None]
        return prod.reshape(rows, LANES)

    nr = lr - 1.0
    den = a_re * a_re + a_im * a_im
    cr = (nr * a_re + li * a_im) / den
    ci = (li * a_re - nr * a_im) / den
    bt = bt_ref[0]
    bbar = bt * cr + _swap_halves(bt) * (sgn1 * ci)
    cc = cc_ref[0]

    lo_big = lax.broadcasted_iota(jnp.int32, (CHUNK_W, LANES), 1) < SSM_STATE
    p_pos = powers(SSM_CHUNK, 6, lr, li)
    cbig = times_coef(p_pos, cc, CHUNK_W)
    wn = (sgn1 * li)[None]
    p_next = (p_pos[0] * lr[None] + p_pos[1] * wn, p_pos[1] * lr[None] - p_pos[0] * wn)
    cl1 = times_coef(p_next, cc, CHUNK_W)
    pt_ref[0] = jnp.where(lo_big, cl1, -cl1).astype(BF16)
    r_ref[0] = times_coef(powers(SSM_CHUNK, 6, lr, li, descending=True), bbar, CHUNK_W).astype(BF16)

    bsmall = times_coef(powers(LANES // SSM_GROUP, 3, ir, ii), bbar, LANES)
    lo8 = lax.broadcasted_iota(jnp.int32, (LANES, LANES), 1) < SSM_STATE
    lhs = jnp.where(lo8, bsmall, -bsmall)
    w2 = _dot_nt(lhs, cbig, precision=lax.Precision.HIGHEST)
    n8 = lax.broadcasted_iota(jnp.int32, (LANES, 1), 0) >> GROUP_SHIFT
    col_t = lax.broadcasted_iota(jnp.int32, (LANES, CHUNK_W), 1) >> GROUP_SHIFT
    w2_ref[0] = jnp.where(col_t >= n8, w2, 0.0).astype(BF16)

    ar, ai = lr, li
    for _ in range(6):
        ar, ai = ar * ar - ai * ai, 2.0 * ar * ai
    a1_ref[0] = ar
    a2_ref[0] = jnp.where(lo1, -ai, ai)


def _ssm_prep(log_dt, a_re, a_im, b_re, b_im, c_re, c_im):
    g = N_GROUPS
    dup = lambda x: jnp.concatenate([x, x], axis=-1).reshape(g, 1, LANES)
    ldt = jnp.broadcast_to(log_dt.reshape(g, 1, 1), (g, 1, LANES))
    bt = jnp.concatenate([b_re.transpose(0, 2, 1), b_im.transpose(0, 2, 1)], axis=-1)
    cc = jnp.concatenate([c_re, c_im], axis=-1)
    vec = pl.BlockSpec((1, 1, LANES), lambda i: (i, 0, 0))
    mat = pl.BlockSpec((1, SSM_GROUP, LANES), lambda i: (i, 0, 0))
    return pl.pallas_call(
        _ssm_prep_kernel,
        grid=(g,),
        in_specs=[vec, vec, vec, mat, mat],
        out_specs=[pl.BlockSpec((1, LANES, CHUNK_W), lambda i: (i, 0, 0)),
                   pl.BlockSpec((1, CHUNK_W, LANES), lambda i: (i, 0, 0)),
                   pl.BlockSpec((1, CHUNK_W, LANES), lambda i: (i, 0, 0)),
                   vec, vec],
        out_shape=[jax.ShapeDtypeStruct((g, LANES, CHUNK_W), BF16),
                   jax.ShapeDtypeStruct((g, CHUNK_W, LANES), BF16),
                   jax.ShapeDtypeStruct((g, CHUNK_W, LANES), BF16),
                   jax.ShapeDtypeStruct((g, 1, LANES), F32),
                   jax.ShapeDtypeStruct((g, 1, LANES), F32)],
        compiler_params=_cparams(("parallel",)),
        name="ssm_prep",
    )(ldt, dup(a_re), dup(a_im), bt, cc)


def _ssm_summary_kernel(u_ref, r_ref, v_ref):
    for gg in range(SSM_GROUPS_PER_STEP):
        v_ref[:, gg * LANES:(gg + 1) * LANES] = _dot(u_ref[gg].astype(BF16), r_ref[gg])


def _ssm_summary(u, r):
    g, m, _ = u.shape
    gs = SSM_GROUPS_PER_STEP
    return pl.pallas_call(
        _ssm_summary_kernel,
        grid=(g // gs,),
        in_specs=[pl.BlockSpec((gs, m, CHUNK_W), lambda i: (i, 0, 0)),
                  pl.BlockSpec((gs, CHUNK_W, LANES), lambda i: (i, 0, 0))],
        out_specs=pl.BlockSpec((m, gs * LANES), lambda i: (0, i)),
        out_shape=jax.ShapeDtypeStruct((m, g * LANES), F32),
        compiler_params=_cparams(("parallel",)),
        name="ssm_summary",
    )(u, r)


def _ssm_scan_kernel(v_ref, a1_ref, a2_ref, s_ref):
    a1 = a1_ref[...]
    a2 = a2_ref[...]
    nk = v_ref.shape[1]

    def step(k, s):
        s_ref[0, k] = s
        return s * a1 + _swap_halves(s) * a2 + v_ref[0, k]

    lax.fori_loop(0, nk, step, jnp.zeros((N_GROUPS, LANES), F32))


def _ssm_scan(v4, a1, a2):
    b, nk = v4.shape[0], v4.shape[1]
    blk = pl.BlockSpec((1, nk, N_GROUPS, LANES), lambda i: (i, 0, 0, 0))
    coef = pl.BlockSpec((N_GROUPS, LANES), lambda i: (0, 0))
    return pl.pallas_call(
        _ssm_scan_kernel,
        grid=(b,),
        in_specs=[blk, coef, coef],
        out_specs=blk,
        out_shape=jax.ShapeDtypeStruct(v4.shape, F32),
        compiler_params=_cparams(("parallel",)),
        name="ssm_scan",
    )(v4, a1, a2)


def _ssm_output_kernel(u_ref, s_ref, w2_ref, pt_ref, d_ref, z_ref, toep_ref):
    @pl.when(pl.program_id(0) == 0)
    def _():
        toep_ref[...] = jnp.zeros_like(toep_ref)

    nblk = CHUNK_W // LANES
    tile = 256
    c0 = math.sqrt(2.0 / math.pi)
    for gg in range(SSM_GROUPS_PER_STEP):
        for i in range(nblk):
            toep_ref[i * LANES:(i + 1) * LANES, i * LANES:] = w2_ref[gg, :, :CHUNK_W - i * LANES]
        u = u_ref[gg]
        ub = u.astype(BF16)
        y = _dot_nt(s_ref[:, gg * LANES:(gg + 1) * LANES].astype(BF16), pt_ref[gg]) + d_ref[gg] * u
        cols = []
        for j in range(CHUNK_W // tile):
            kk = (j + 1) * tile
            cols.append(_dot(ub[:, :kk], toep_ref[:kk, j * tile:(j + 1) * tile]))
        y = y + jnp.concatenate(cols, axis=1)
        z_ref[gg] = 0.5 * y * (1.0 + jnp.tanh(c0 * (y + 0.044715 * (y * y * y))))


def _ssm_output(u, s, w2, pt, d_t):
    g, m, _ = u.shape
    gs = SSM_GROUPS_PER_STEP
    return pl.pallas_call(
        _ssm_output_kernel,
        grid=(g // gs,),
        in_specs=[pl.BlockSpec((gs, m, CHUNK_W), lambda i: (i, 0, 0)),
                  pl.BlockSpec((m, gs * LANES), lambda i: (0, i)),
                  pl.BlockSpec((gs, LANES, CHUNK_W), lambda i: (i, 0, 0)),
                  pl.BlockSpec((gs, CHUNK_W, LANES), lambda i: (i, 0, 0)),
                  pl.BlockSpec((gs, 1, CHUNK_W), lambda i: (i, 0, 0))],
        out_specs=pl.BlockSpec((gs, m, CHUNK_W), lambda i: (i, 0, 0)),
        out_shape=jax.ShapeDtypeStruct((g, m, CHUNK_W), F32),
        scratch_shapes=[pltpu.VMEM((CHUNK_W, CHUNK_W), BF16)],
        compiler_params=_cparams(("arbitrary",)),
        name="ssm_output",
    )(u, s, w2, pt, d_t)


def _glu_kernel(h_ref, z_ref, w_ref, o_ref, z_scr):
    zw = None
    for first in range(0, LANE_BLOCKS, GLU_SLAB):
        blocks = tuple(range(first, first + GLU_SLAB))
        _unpack_chunks(z_ref, z_scr, blocks)
        z = jnp.concatenate([z_scr[gb].astype(BF16) for gb in blocks], axis=1)
        part = _dot(z, w_ref[first * LANES:(first + GLU_SLAB) * LANES, :])
        zw = part if zw is None else zw + part
    o_ref[...] = h_ref[...] + zw[:, :D_MODEL] * jax.nn.sigmoid(zw[:, D_MODEL:])


def _glu(h2, z, w):
    m = h2.shape[0]
    return pl.pallas_call(
        _glu_kernel,
        grid=(m // ROW_TILE,),
        in_specs=[pl.BlockSpec((ROW_TILE, D_MODEL), lambda i: (i, 0)),
                  pl.BlockSpec((N_GROUPS, PACK_CHUNKS, CHUNK_W), lambda i: (0, i, 0)),
                  pl.BlockSpec(w.shape, lambda i: (0, 0))],
        out_specs=pl.BlockSpec((ROW_TILE, D_MODEL), lambda i: (i, 0)),
        out_shape=jax.ShapeDtypeStruct((m, D_MODEL), F32),
        scratch_shapes=[pltpu.VMEM((LANE_BLOCKS, ROW_TILE, LANES), F32)],
        compiler_params=_cparams(("parallel",)),
        name="glu",
    )(h2, z, w)


def _mlp_kernel(h_ref, g_ref, w1_ref, w2_ref, fg_ref, *rest, final_norm, with_proj):
    if with_proj:
        a_ref, wo_ref, o_ref = rest
        x = h_ref[...] + _dot(a_ref[...], wo_ref[...])
    else:
        (o_ref,) = rest
        x = h_ref[...]
    xn = _rms(x, g_ref[...]).astype(BF16)
    acc = x
    for f in range(0, D_FF, FF_CHUNK):
        a = jnp.square(jnp.maximum(_dot(xn, w1_ref[:, f:f + FF_CHUNK]), 0.0))
        acc = acc + _dot(a.astype(BF16), w2_ref[f:f + FF_CHUNK, :])
    if final_norm:
        acc = _rms(acc, fg_ref[...])
    o_ref[...] = acc


def _mlp(h2, g, w1, w2, fg, final_norm, proj=None):
    m = h2.shape[0]
    once = dict(pipeline_mode=pl.Buffered(1))
    in_specs = [pl.BlockSpec((ROW_TILE, D_MODEL), lambda i: (i, 0)),
                pl.BlockSpec((1, D_MODEL), lambda i: (0, 0)),
                pl.BlockSpec((D_MODEL, D_FF), lambda i: (0, 0), **once),
                pl.BlockSpec((D_FF, D_MODEL), lambda i: (0, 0), **once),
                pl.BlockSpec((1, D_MODEL), lambda i: (0, 0))]
    args = [h2, g.reshape(1, D_MODEL), w1, w2, fg.reshape(1, D_MODEL)]
    if proj is not None:
        a, wo = proj
        in_specs += [pl.BlockSpec((ROW_TILE, a.shape[1]), lambda i: (i, 0)),
                     pl.BlockSpec(wo.shape, lambda i: (0, 0), **once)]
        args += [a, wo]
    return pl.pallas_call(
        functools.partial(_mlp_kernel, final_norm=final_norm, with_proj=proj is not None),
        grid=(m // ROW_TILE,),
        in_specs=in_specs,
        out_specs=pl.BlockSpec((ROW_TILE, D_MODEL), lambda i: (i, 0)),
        out_shape=jax.ShapeDtypeStruct((m, D_MODEL), F32),
        compiler_params=_cparams(("parallel",)),
        name="mlp_final" if final_norm else "mlp",
    )(*args)


def _extra_scatter(sign_f, f_first):
    scat = np.zeros((LANES, N_HEADS * HEAD_PAD), np.float32)
    const = np.zeros((1, N_HEADS * HEAD_PAD), np.float32)
    for h in range(N_HEADS):
        base = h * HEAD_PAD + HEAD_DIM
        f0, o0 = (0, N_EXTRA) if f_first else (N_EXTRA, 0)
        for j in range(N_EXTRA):
            scat[j * N_HEADS + h, base + f0 + j] = sign_f
            const[0, base + o0 + j] = 1.0
    return jnp.asarray(scat, BF16), jnp.asarray(const, F32)


def _augment(proj, f_log2, sc_ref, cst_ref, out_ref):
    hi = f_log2.astype(BF16).astype(F32)
    r1 = f_log2 - hi
    mid = r1.astype(BF16).astype(F32)
    lo = (r1 - mid).astype(BF16).astype(F32)
    terms = hi + pltpu.roll(mid, N_HEADS, axis=1) + pltpu.roll(lo, 2 * N_HEADS, axis=1)
    extra = _dot(terms.astype(BF16), sc_ref[...]) + cst_ref[...]
    lane = lax.broadcasted_iota(jnp.int32, (proj.shape[0], LANES), 1)
    for hh in range(N_HEADS):
        blk = proj[:, (hh // 2) * LANES:(hh // 2 + 1) * LANES]
        if hh % 2:
            blk = pltpu.roll(blk, HEAD_DIM, axis=1)
        out_ref[0, hh] = jnp.where(lane < HEAD_DIM, blk,
                                   extra[:, hh * HEAD_PAD:(hh + 1) * HEAD_PAD]).astype(BF16)


def _kv_kernel(h_ref, g_ref, wk_ref, wv_ref, vone_ref, wf_ref, bf_ref, sc_ref, cst_ref,
               k_ref, v_ref, f_ref, carry_ref):
    @pl.when(pl.program_id(1) == 0)
    def _():
        carry_ref[...] = jnp.zeros_like(carry_ref)

    xn = _rms(h_ref[0], g_ref[...]).astype(BF16)
    v_ref[0, 0] = (_dot_nt(wv_ref[...], xn) + vone_ref[...]).astype(BF16)

    logit = _dot(xn, wf_ref[...]) + bf_ref[...]
    log_f = jnp.minimum(logit, 0.0) - jnp.log1p(jnp.exp(-jnp.abs(logit)))
    lane = lax.broadcasted_iota(jnp.int32, log_f.shape, 1)
    log_f = jnp.where(lane < N_HEADS, log_f, 0.0)
    t = log_f.shape[0]
    tri = (lax.broadcasted_iota(jnp.int32, (t, t), 0)
           >= lax.broadcasted_iota(jnp.int32, (t, t), 1)).astype(BF16)
    hi, mid, lo = _split3(log_f)
    cum = (_dot(tri, hi) + _dot(tri, mid)) + _dot(tri, lo) + carry_ref[...]
    carry_ref[...] = cum[t - 1:t, :]
    f_ref[0] = cum

    _augment(_dot(xn, wk_ref[...]), cum * LOG2E, sc_ref, cst_ref, k_ref)


def _kv_proj(h3, g, wk, wv, wf, bf):
    b, l, _ = h3.shape
    nt = l // ROW_TILE
    scat, const = _extra_scatter(-1.0, f_first=False)
    vone = np.zeros((N_HEADS, VT_ROWS, 1), np.float32)
    vone[:, HEAD_DIM] = 1.0
    vone = jnp.asarray(vone.reshape(N_HEADS * VT_ROWS, 1))
    full = lambda a: pl.BlockSpec(a.shape, lambda i, j: (0,) * a.ndim)
    return pl.pallas_call(
        _kv_kernel,
        grid=(b, nt),
        in_specs=[pl.BlockSpec((1, ROW_TILE, D_MODEL), lambda i, j: (i, j, 0)),
                  pl.BlockSpec((1, D_MODEL), lambda i, j: (0, 0)),
                  full(wk), full(wv), full(vone), full(wf), full(bf), full(scat), full(const)],
        out_specs=[pl.BlockSpec((1, N_HEADS, ROW_TILE, HEAD_PAD), lambda i, j: (i, 0, j, 0)),
                   pl.BlockSpec((1, 1, N_HEADS * VT_ROWS, ROW_TILE), lambda i, j: (i, j, 0, 0)),
                   pl.BlockSpec((1, ROW_TILE, LANES), lambda i, j: (i, j, 0))],
        out_shape=[jax.ShapeDtypeStruct((b, N_HEADS, l, HEAD_PAD), BF16),
                   jax.ShapeDtypeStruct((b, nt, N_HEADS * VT_ROWS, ROW_TILE), BF16),
                   jax.ShapeDtypeStruct((b, l, LANES), F32)],
        scratch_shapes=[pltpu.VMEM((1, LANES), F32)],
        compiler_params=_cparams(("parallel", "arbitrary")),
        name="kv_proj",
    )(h3, g.reshape(1, D_MODEL), wk, wv, vone, wf, bf, scat, const)


def _q_kernel(h_ref, g_ref, wq_ref, f_ref, sc_ref, cst_ref, q_ref):
    xn = _rms(h_ref[0], g_ref[...]).astype(BF16)
    _augment(_dot(xn, wq_ref[...]), f_ref[0] * LOG2E, sc_ref, cst_ref, q_ref)


def _q_proj(h3, g, wq, fcum):
    b, l, _ = h3.shape
    scat, const = _extra_scatter(1.0, f_first=True)
    full = lambda a: pl.BlockSpec(a.shape, lambda i, j: (0,) * a.ndim)
    return pl.pallas_call(
        _q_kernel,
        grid=(b, l // ROW_TILE),
        in_specs=[pl.BlockSpec((1, ROW_TILE, D_MODEL), lambda i, j: (i, j, 0)),
                  pl.BlockSpec((1, D_MODEL), lambda i, j: (0, 0)),
                  full(wq),
                  pl.BlockSpec((1, ROW_TILE, LANES), lambda i, j: (i, j, 0)),
                  full(scat), full(const)],
        out_specs=pl.BlockSpec((1, N_HEADS, ROW_TILE, HEAD_PAD), lambda i, j: (i, 0, j, 0)),
        out_shape=jax.ShapeDtypeStruct((b, N_HEADS, l, HEAD_PAD), BF16),
        compiler_params=_cparams(("parallel", "parallel")),
        name="q_proj",
    )(h3, g.reshape(1, D_MODEL), wq, fcum, scat, const)


def _causal_items(length):
    return sum((qi * ATT_BQ) // ATT_BK + 1 for qi in range(length // ATT_BQ))


def _diag_bias():
    k = np.arange(ATT_BK)[:, None]
    q = np.arange(ATT_BQ)[None, :]
    tabs = [np.zeros((ATT_BK, ATT_BQ), np.float32)]
    for r in range(ATT_BK // ATT_BQ):
        tabs.append(np.where(k - q <= r * ATT_BQ, 0.0, NEG_BIG).astype(np.float32))
    return jnp.asarray(np.stack(tabs))


def _attn_kernel(q_ref, k_ref, vt_ref, bias_ref, o_ref, *scratch):
    s_scr = scratch[:ATT_SLOTS]
    p_scr = scratch[ATT_SLOTS:2 * ATT_SLOTS]
    acc_scr = scratch[2 * ATT_SLOTS]
    length = q_ref.shape[2]
    n_q = length // ATT_BQ
    n_items = _causal_items(length)
    assert n_items % ATT_SLOTS == 0

    def blocks_of(qi):
        return (qi * ATT_BQ) // ATT_BK + 1

    def advance(item):
        qi, j = item
        last = j + 1 == blocks_of(qi)
        return (jnp.minimum(jnp.where(last, qi + 1, qi), n_q - 1), jnp.where(last, 0, j + 1))

    def step(slot, item_s, item_x, item_v, chain, lane):
        m, acc = chain
        mb, alpha_q = lane
        out_chain, out_lane = [], []
        qi_v, j_v = item_v
        for hh in range(ATT_HEADS):
            vt = vt_ref[0, j_v, pl.ds(hh * VT_ROWS, VT_ROWS), :]
            acc_h = alpha_q[hh] * acc[hh] + _dot(vt, p_scr[slot][hh])
            acc_scr[qi_v, hh] = acc_h
            out_chain.append([None, acc_h])
        _, j_x = item_x
        for hh in range(ATT_HEADS):
            m_in = jnp.where(j_x == 0, NEG_BIG, m[hh])
            m_new = jnp.maximum(m_in, mb[hh])
            p_scr[slot][hh] = jnp.exp2(s_scr[slot][hh] - m_new).astype(BF16)
            out_chain[hh][0] = m_new
            out_lane.append([None, jnp.exp2(m_in - m_new)])
        qi_s, j_s = item_s
        shift = qi_s * ATT_BQ - j_s * ATT_BK
        sel = jnp.where(j_s + 1 == blocks_of(qi_s), 1 + shift // ATT_BQ, 0)
        for hh in range(ATT_HEADS):
            s = _dot_nt(k_ref[0, hh, pl.ds(j_s * ATT_BK, ATT_BK), :],
                        q_ref[0, hh, pl.ds(qi_s * ATT_BQ, ATT_BQ), :]) + bias_ref[sel]
            s_scr[slot][hh] = s
            out_lane[hh][0] = jnp.max(s, axis=0, keepdims=True)
        chain = tuple(tuple(c[i] for c in out_chain) for i in range(2))
        lane = tuple(tuple(c[i] for c in out_lane) for i in range(2))
        return chain, lane

    for slot in range(ATT_SLOTS):
        s_scr[slot][...] = jnp.full(s_scr[slot].shape, NULL_KEY, F32)
        p_scr[slot][...] = jnp.zeros(p_scr[slot].shape, BF16)

    def per_head(shape, v):
        pos = lax.broadcasted_iota(jnp.int32, shape, 0) + lax.broadcasted_iota(jnp.int32, shape, 1)
        return tuple(jnp.where(pos >= 0, v, 0.0).astype(F32) for _ in range(ATT_HEADS))

    row = (1, ATT_BQ)
    chain = (per_head(row, NEG_BIG), per_head((VT_ROWS, ATT_BQ), 0.0))
    lane = (per_head(row, NULL_KEY), per_head(row, 1.0))
    zero = jnp.int32(0)
    first = (zero, zero)

    def body(_, carry):
        item, hists, chain, lanes = carry
        new_hists, new_lanes = [], []
        for slot in range(ATT_SLOTS):
            item_x, item_v = hists[slot]
            chain, lane = step(slot, item, item_x, item_v, chain, lanes[slot])
            new_hists.append((item, item_x))
            new_lanes.append(lane)
            item = advance(item)
        return item, tuple(new_hists), chain, tuple(new_lanes)

    lax.fori_loop(0, n_items // ATT_SLOTS + 2, body,
                  (first, ((first, first),) * ATT_SLOTS, chain, (lane,) * ATT_SLOTS))

    def finish(qi, _):
        outs = []
        for hh in range(ATT_HEADS):
            a = acc_scr[qi, hh]
            outs.append(a[:HEAD_DIM] / a[HEAD_DIM:HEAD_DIM + 1])
        o_ref[0, pl.ds(qi * ATT_BQ, ATT_BQ), :] = jnp.concatenate(outs, axis=0).T.astype(BF16)
        return 0

    lax.fori_loop(0, n_q, finish, 0, unroll=FINISH_UNROLL)


def _attention(q_aug, k_aug, vt):
    b, _, l, _ = q_aug.shape
    bias = _diag_bias()
    return pl.pallas_call(
        _attn_kernel,
        grid=(b, N_HEADS // ATT_HEADS),
        in_specs=[pl.BlockSpec((1, ATT_HEADS, l, HEAD_PAD), lambda i, h: (i, h, 0, 0)),
                  pl.BlockSpec((1, ATT_HEADS, l, HEAD_PAD), lambda i, h: (i, h, 0, 0)),
                  pl.BlockSpec((1, l // ATT_BK, ATT_HEADS * VT_ROWS, ATT_BK), lambda i, h: (i, 0, h, 0)),
                  pl.BlockSpec(bias.shape, lambda i, h: (0, 0, 0))],
        out_specs=pl.BlockSpec((1, l, ATT_HEADS * HEAD_DIM), lambda i, h: (i, 0, h)),
        out_shape=jax.ShapeDtypeStruct((b, l, D_MODEL), BF16),
        scratch_shapes=[pltpu.VMEM((ATT_HEADS, ATT_BK, ATT_BQ), F32)] * ATT_SLOTS
                       + [pltpu.VMEM((ATT_HEADS, ATT_BK, ATT_BQ), BF16)] * ATT_SLOTS
                       + [pltpu.VMEM((l // ATT_BQ, ATT_HEADS, VT_ROWS, ATT_BQ), F32)],
        compiler_params=_cparams(("parallel", "parallel")),
        name="fox_attention",
    )(q_aug, k_aug, vt, bias)


def kernel(x, mix_norm, mlp_norm, mlp_w1, mlp_w2, ssm_log_dt, ssm_a_re, ssm_a_im,
           ssm_b_re, ssm_b_im, ssm_c_re, ssm_c_im, ssm_d, ssm_w_glu, kv_norm, w_kvf, b_f,
           attn_wq, attn_wo, final_norm):
    bsz, length, _ = x.shape
    m = bsz * length
    nk = length // SSM_CHUNK
    h = x.reshape(m, D_MODEL)

    for i in range(DEPTH):
        if i < N_A_LAYERS:
            w2, pt, r, a1, a2 = _ssm_prep(ssm_log_dt[i], ssm_a_re[i], ssm_a_im[i], ssm_b_re[i],
                                          ssm_b_im[i], ssm_c_re[i], ssm_c_im[i])
            u = _norm_pack(h, mix_norm[i])
            v = _ssm_summary(u, r)
            s = _ssm_scan(v.reshape(bsz, nk, N_GROUPS, LANES),
                          a1.reshape(N_GROUPS, LANES), a2.reshape(N_GROUPS, LANES))
            d_t = jnp.tile(ssm_d[i].reshape(N_GROUPS, 1, SSM_GROUP), (1, 1, SSM_CHUNK))
            z = _ssm_output(u, s.reshape(bsz * nk, N_GROUPS * LANES), w2, pt, d_t)
            h = _glu(h, z, ssm_w_glu[i].astype(BF16))
            proj = None
        else:
            j = i - N_A_LAYERS
            wq = (attn_wq[j] * (HEAD_DIM ** -0.5 * LOG2E)).astype(BF16)
            q_aug = _q_proj(h.reshape(bsz, length, D_MODEL), mix_norm[i], wq, fcum)
            o = _attention(q_aug, k_aug, v_nat)
            proj = (o.reshape(m, D_MODEL), attn_wo[j].astype(BF16))
        h = _mlp(h, mlp_norm[i], mlp_w1[i].astype(BF16), mlp_w2[i].astype(BF16),
                 final_norm, final_norm=(i == DEPTH - 1), proj=proj)
        if i == N_A_LAYERS - 1:
            attn_dim = N_HEADS * HEAD_DIM
            wk = w_kvf[:, :attn_dim].astype(BF16)
            wv = w_kvf[:, attn_dim:2 * attn_dim].T.reshape(N_HEADS, HEAD_DIM, D_MODEL)
            wv = jnp.pad(wv, ((0, 0), (0, VT_ROWS - HEAD_DIM), (0, 0)))
            wv = wv.reshape(N_HEADS * VT_ROWS, D_MODEL).astype(BF16)
            wf = jnp.pad(w_kvf[:, 2 * attn_dim:], ((0, 0), (0, LANES - N_HEADS))).astype(BF16)
            bf = jnp.pad(b_f, (0, LANES - N_HEADS)).reshape(1, LANES)
            k_aug, v_nat, fcum = _kv_proj(h.reshape(bsz, length, D_MODEL), kv_norm,
                                          wk, wv, wf, bf)
    return h.reshape(bsz, length, D_MODEL)
```

```python
import functools
import math

import numpy as np
import jax
import jax.numpy as jnp
from jax import lax
from jax.experimental import pallas as pl
from jax.experimental.pallas import tpu as pltpu

D_MODEL = 1024
N_GROUPS = 64
SSM_GROUP = 16
GROUP_SHIFT = 4
SSM_STATE = 64
N_HEADS = 16
HEAD_DIM = 64
D_FF = 4 * D_MODEL
RMS_EPS = 1e-6
N_A_LAYERS = 2
DEPTH = 4

SSM_CHUNK = 64
CHUNK_W = SSM_CHUNK * SSM_GROUP
LANES = 128
HEAD_PAD = 128
N_EXTRA = 3

ROW_TILE = 512
FF_CHUNK = 1024
ATT_BQ = 256
VT_ROWS = 80
ATT_SLOTS = 8
SSM_GROUPS_PER_STEP = 4
GLU_SLAB = 2
FINISH_UNROLL = 4
ATT_HEADS = 2
ATT_BK = 512
NEG_BIG = -1e30
NULL_KEY = 3.0 * NEG_BIG
LOG2E = math.log2(math.e)

assert ROW_TILE == ATT_BK and ATT_BK % ATT_BQ == 0

VMEM_LIMIT = 56 * 1024 * 1024

F32 = jnp.float32
BF16 = jnp.bfloat16


def _cparams(sem):
    return pltpu.CompilerParams(dimension_semantics=sem, vmem_limit_bytes=VMEM_LIMIT)


def _rms(x, g):
    return x * lax.rsqrt(jnp.mean(x * x, axis=-1, keepdims=True) + RMS_EPS) * g


def _dot(a, b):
    return jnp.dot(a, b, preferred_element_type=F32)


def _dot_nt(a, b, precision=None):
    return lax.dot_general(a, b, (((1,), (1,)), ((), ())),
                           preferred_element_type=F32, precision=precision)


def _split3(x):
    hi = x.astype(BF16)
    r1 = x - hi.astype(F32)
    mid = r1.astype(BF16)
    lo = (r1 - mid.astype(F32)).astype(BF16)
    return hi, mid, lo


PACK_CHUNKS = 8
PIECES = LANES // SSM_GROUP
assert PACK_CHUNKS * SSM_CHUNK == ROW_TILE and PIECES == 8


def _piece_transpose(blocks):
    piece = lax.broadcasted_iota(jnp.int32, (PACK_CHUNKS, LANES), 1) >> GROUP_SHIFT
    for d in (4, 2, 1):
        keep = (piece & d) == 0
        nxt = []
        for v in blocks:
            out = list(v)
            for a in range(PIECES):
                if a & d:
                    continue
                b = a + d
                out[a] = jnp.where(keep, v[a], pltpu.roll(v[b], SSM_GROUP * d, axis=1))
                out[b] = jnp.where(keep, pltpu.roll(v[a], LANES - SSM_GROUP * d, axis=1), v[b])
            nxt.append(out)
        blocks = nxt
    return blocks


LANE_BLOCKS = D_MODEL // LANES


def _pack_chunks(tok_ref, grp_ref):
    n_sb = SSM_CHUNK // PIECES
    for gb in range(LANE_BLOCKS):
        v = [[tok_ref[gb, pl.ds(PIECES * sb + j, PACK_CHUNKS, stride=SSM_CHUNK), :]
              for j in range(PIECES)] for sb in range(n_sb)]
        w = _piece_transpose(v)
        for sb in range(n_sb):
            for i in range(PIECES):
                grp_ref[PIECES * gb + i, :, sb * LANES:(sb + 1) * LANES] = w[sb][i]


def _unpack_chunks(grp_ref, tok_ref, lane_blocks):
    n_sb = SSM_CHUNK // PIECES
    for gb in lane_blocks:
        w = [[grp_ref[PIECES * gb + i, :, sb * LANES:(sb + 1) * LANES] for i in range(PIECES)]
             for sb in range(n_sb)]
        v = _piece_transpose(w)
        for sb in range(n_sb):
            for j in range(PIECES):
                tok_ref[gb, pl.ds(PIECES * sb + j, PACK_CHUNKS, stride=SSM_CHUNK), :] = v[sb][j]


def _norm_pack_kernel(h_ref, g_ref, u_ref, xn_scr):
    xn = _rms(h_ref[...], g_ref[...])
    for gb in range(LANE_BLOCKS):
        xn_scr[gb] = xn[:, gb * LANES:(gb + 1) * LANES]
    _pack_chunks(xn_scr, u_ref)


def _norm_pack(h2, g):
    m = h2.shape[0]
    return pl.pallas_call(
        _norm_pack_kernel,
        grid=(m // ROW_TILE,),
        in_specs=[pl.BlockSpec((ROW_TILE, D_MODEL), lambda i: (i, 0)),
                  pl.BlockSpec((1, D_MODEL), lambda i: (0, 0))],
        out_specs=pl.BlockSpec((N_GROUPS, PACK_CHUNKS, CHUNK_W), lambda i: (0, i, 0)),
        out_shape=jax.ShapeDtypeStruct((N_GROUPS, m // SSM_CHUNK, CHUNK_W), F32),
        scratch_shapes=[pltpu.VMEM((LANE_BLOCKS, ROW_TILE, LANES), F32)],
        compiler_params=_cparams(("parallel",)),
        name="norm_pack",
    )(h2, g.reshape(1, D_MODEL))


def _swap_halves(x):
    return pltpu.roll(x, 64, axis=x.ndim - 1)


def _ssm_prep_kernel(ldt_ref, are_ref, aim_ref, bt_ref, cc_ref,
                     w2_ref, pt_ref, r_ref, a1_ref, a2_ref):
    lane1 = lax.broadcasted_iota(jnp.int32, (1, LANES), 1)
    lo1 = lane1 < SSM_STATE
    a_re = are_ref[0]
    a_im = aim_ref[0]
    dt = jnp.exp(ldt_ref[0])
    zr = a_re * dt
    zi = a_im * dt
    mag = jnp.exp(zr)
    lr = mag * jnp.cos(zi)
    li = mag * jnp.sin(zi)
    imag = jnp.exp(-zr)
    ir = imag * jnp.cos(zi)
    ii = -imag * jnp.sin(zi)

    sgn1 = jnp.where(lo1, -1.0, 1.0)

    def powers(count, nbits, wr, wi, descending=False):
        n = lax.broadcasted_iota(jnp.int32, (count, 1, LANES), 0)
        if descending:
            n = count - 1 - n
        lo3 = lax.broadcasted_iota(jnp.int32, (count, 1, LANES), 2) < SSM_STATE
        p = jnp.where(lo3, 1.0, 0.0).astype(F32)
        ps = jnp.where(lo3, 0.0, 1.0).astype(F32)
        for k in range(nbits):
            bit = ((n >> k) & 1) == 1
            wn = (sgn1 * wi)[None]
            p, ps = (jnp.where(bit, p * wr[None] + ps * wn, p),
                     jnp.where(bit, ps * wr[None] - p * wn, ps))
            wr, wi = wr * wr - wi * wi, 2.0 * wr * wi
        return p, ps

    def times_coef(table, coef, rows):
        p, ps = table
        cs = _swap_halves(coef)
        lo = lax.broadcasted_iota(jnp.int32, coef.shape, 1) < SSM_STATE
        prod = p * jnp.where(lo, coef, cs)[None] + ps * jnp.where(lo, -cs, coef)[None]
        return prod.reshape(rows, LANES)

    nr = lr - 1.0
    den = a_re * a_re + a_im * a_im
    cr = (nr * a_re + li * a_im) / den
    ci = (li * a_re - nr * a_im) / den
    bt = bt_ref[0]
    bbar = bt * cr + _swap_halves(bt) * (sgn1 * ci)
    cc = cc_ref[0]

    lo_big = lax.broadcasted_iota(jnp.int32, (CHUNK_W, LANES), 1) < SSM_STATE
    p_pos = powers(SSM_CHUNK, 6, lr, li)
    cbig = times_coef(p_pos, cc, CHUNK_W)
    wn = (sgn1 * li)[None]
    p_next = (p_pos[0] * lr[None] + p_pos[1] * wn, p_pos[1] * lr[None] - p_pos[0] * wn)
    cl1 = times_coef(p_next, cc, CHUNK_W)
    pt_ref[0] = jnp.where(lo_big, cl1, -cl1).astype(BF16)
    r_ref[0] = times_coef(powers(SSM_CHUNK, 6, lr, li, descending=True), bbar, CHUNK_W).astype(BF16)

    bsmall = times_coef(powers(LANES // SSM_GROUP, 3, ir, ii), bbar, LANES)
    lo8 = lax.broadcasted_iota(jnp.int32, (LANES, LANES), 1) < SSM_STATE
    lhs = jnp.where(lo8, bsmall, -bsmall)
    w2 = _dot_nt(lhs, cbig, precision=lax.Precision.HIGHEST)
    n8 = lax.broadcasted_iota(jnp.int32, (LANES, 1), 0) >> GROUP_SHIFT
    col_t = lax.broadcasted_iota(jnp.int32, (LANES, CHUNK_W), 1) >> GROUP_SHIFT
    w2_ref[0] = jnp.where(col_t >= n8, w2, 0.0).astype(BF16)

    ar, ai = lr, li
    for _ in range(6):
        ar, ai = ar * ar - ai * ai, 2.0 * ar * ai
    a1_ref[0] = ar
    a2_ref[0] = jnp.where(lo1, -ai, ai)


def _ssm_prep(log_dt, a_re, a_im, b_re, b_im, c_re, c_im):
    g = N_GROUPS
    dup = lambda x: jnp.concatenate([x, x], axis=-1).reshape(g, 1, LANES)
    ldt = jnp.broadcast_to(log_dt.reshape(g, 1, 1), (g, 1, LANES))
    bt = jnp.concatenate([b_re.transpose(0, 2, 1), b_im.transpose(0, 2, 1)], axis=-1)
    cc = jnp.concatenate([c_re, c_im], axis=-1)
    vec = pl.BlockSpec((1, 1, LANES), lambda i: (i, 0, 0))
    mat = pl.BlockSpec((1, SSM_GROUP, LANES), lambda i: (i, 0, 0))
    return pl.pallas_call(
        _ssm_prep_kernel,
        grid=(g,),
        in_specs=[vec, vec, vec, mat, mat],
        out_specs=[pl.BlockSpec((1, LANES, CHUNK_W), lambda i: (i, 0, 0)),
                   pl.BlockSpec((1, CHUNK_W, LANES), lambda i: (i, 0, 0)),
                   pl.BlockSpec((1, CHUNK_W, LANES), lambda i: (i, 0, 0)),
                   vec, vec],
        out_shape=[jax.ShapeDtypeStruct((g, LANES, CHUNK_W), BF16),
                   jax.ShapeDtypeStruct((g, CHUNK_W, LANES), BF16),
                   jax.ShapeDtypeStruct((g, CHUNK_W, LANES), BF16),
                   jax.ShapeDtypeStruct((g, 1, LANES), F32),
                   jax.ShapeDtypeStruct((g, 1, LANES), F32)],
        compiler_params=_cparams(("parallel",)),
        name="ssm_prep",
    )(ldt, dup(a_re), dup(a_im), bt, cc)


def _ssm_summary_kernel(u_ref, r_ref, v_ref):
    for gg in range(SSM_GROUPS_PER_STEP):
        v_ref[:, gg * LANES:(gg + 1) * LANES] = _dot(u_ref[gg].astype(BF16), r_ref[gg])


def _ssm_summary(u, r):
    g, m, _ = u.shape
    gs = SSM_GROUPS_PER_STEP
    return pl.pallas_call(
        _ssm_summary_kernel,
        grid=(g // gs,),
        in_specs=[pl.BlockSpec((gs, m, CHUNK_W), lambda i: (i, 0, 0)),
                  pl.BlockSpec((gs, CHUNK_W, LANES), lambda i: (i, 0, 0))],
        out_specs=pl.BlockSpec((m, gs * LANES), lambda i: (0, i)),
        out_shape=jax.ShapeDtypeStruct((m, g * LANES), F32),
        compiler_params=_cparams(("parallel",)),
        name="ssm_summary",
    )(u, r)


def _ssm_scan_kernel(v_ref, a1_ref, a2_ref, s_ref):
    a1 = a1_ref[...]
    a2 = a2_ref[...]
    nk = v_ref.shape[1]

    def step(k, s):
        s_ref[0, k] = s
        return s * a1 + _swap_halves(s) * a2 + v_ref[0, k]

    lax.fori_loop(0, nk, step, jnp.zeros((N_GROUPS, LANES), F32))


def _ssm_scan(v4, a1, a2):
    b, nk = v4.shape[0], v4.shape[1]
    blk = pl.BlockSpec((1, nk, N_GROUPS, LANES), lambda i: (i, 0, 0, 0))
    coef = pl.BlockSpec((N_GROUPS, LANES), lambda i: (0, 0))
    return pl.pallas_call(
        _ssm_scan_kernel,
        grid=(b,),
        in_specs=[blk, coef, coef],
        out_specs=blk,
        out_shape=jax.ShapeDtypeStruct(v4.shape, F32),
        compiler_params=_cparams(("parallel",)),
        name="ssm_scan",
    )(v4, a1, a2)


def _ssm_output_kernel(u_ref, s_ref, w2_ref, pt_ref, d_ref, z_ref, toep_ref):
    @pl.when(pl.program_id(0) == 0)
    def _():
        toep_ref[...] = jnp.zeros_like(toep_ref)

    nblk = CHUNK_W // LANES
    tile = 256
    c0 = math.sqrt(2.0 / math.pi)
    for gg in range(SSM_GROUPS_PER_STEP):
        for i in range(nblk):
            toep_ref[i * LANES:(i + 1) * LANES, i * LANES:] = w2_ref[gg, :, :CHUNK_W - i * LANES]
        u = u_ref[gg]
        ub = u.astype(BF16)
        y = _dot_nt(s_ref[:, gg * LANES:(gg + 1) * LANES].astype(BF16), pt_ref[gg]) + d_ref[gg] * u
        cols = []
        for j in range(CHUNK_W // tile):
            kk = (j + 1) * tile
            cols.append(_dot(ub[:, :kk], toep_ref[:kk, j * tile:(j + 1) * tile]))
        y = y + jnp.concatenate(cols, axis=1)
        z_ref[gg] = 0.5 * y * (1.0 + jnp.tanh(c0 * (y + 0.044715 * (y * y * y))))


def _ssm_output(u, s, w2, pt, d_t):
    g, m, _ = u.shape
    gs = SSM_GROUPS_PER_STEP
    return pl.pallas_call(
        _ssm_output_kernel,
        grid=(g // gs,),
        in_specs=[pl.BlockSpec((gs, m, CHUNK_W), lambda i: (i, 0, 0)),
                  pl.BlockSpec((m, gs * LANES), lambda i: (0, i)),
                  pl.BlockSpec((gs, LANES, CHUNK_W), lambda i: (i, 0, 0)),
                  pl.BlockSpec((gs, CHUNK_W, LANES), lambda i: (i, 0, 0)),
                  pl.BlockSpec((gs, 1, CHUNK_W), lambda i: (i, 0, 0))],
        out_specs=pl.BlockSpec((gs, m, CHUNK_W), lambda i: (i, 0, 0)),
        out_shape=jax.ShapeDtypeStruct((g, m, CHUNK_W), F32),
        scratch_shapes=[pltpu.VMEM((CHUNK_W, CHUNK_W), BF16)],
        compiler_params=_cparams(("arbitrary",)),
        name="ssm_output",
    )(u, s, w2, pt, d_t)


def _glu_kernel(h_ref, z_ref, w_ref, o_ref, z_scr):
    zw = None
    for first in range(0, LANE_BLOCKS, GLU_SLAB):
        blocks = tuple(range(first, first + GLU_SLAB))
        _unpack_chunks(z_ref, z_scr, blocks)
        z = jnp.concatenate([z_scr[gb].astype(BF16) for gb in blocks], axis=1)
        part = _dot(z, w_ref[first * LANES:(first + GLU_SLAB) * LANES, :])
        zw = part if zw is None else zw + part
    o_ref[...] = h_ref[...] + zw[:, :D_MODEL] * jax.nn.sigmoid(zw[:, D_MODEL:])


def _glu(h2, z, w, layer):
    m = h2.shape[0]
    return pl.pallas_call(
        _glu_kernel,
        grid=(m // ROW_TILE,),
        in_specs=[pl.BlockSpec((ROW_TILE, D_MODEL), lambda i: (i, 0)),
                  pl.BlockSpec((N_GROUPS, PACK_CHUNKS, CHUNK_W), lambda i: (0, i, 0)),
                  pl.BlockSpec((None,) + w.shape[1:], lambda i: (layer, 0, 0))],
        out_specs=pl.BlockSpec((ROW_TILE, D_MODEL), lambda i: (i, 0)),
        out_shape=jax.ShapeDtypeStruct((m, D_MODEL), F32),
        scratch_shapes=[pltpu.VMEM((LANE_BLOCKS, ROW_TILE, LANES), F32)],
        compiler_params=_cparams(("parallel",)),
        name="glu",
    )(h2, z, w)


def _mlp_kernel(h_ref, g_ref, w1_ref, w2_ref, fg_ref, *rest, final_norm, with_proj):
    if with_proj:
        a_ref, wo_ref, o_ref = rest
        x = h_ref[...] + _dot(a_ref[...], wo_ref[...])
    else:
        (o_ref,) = rest
        x = h_ref[...]
    xn = _rms(x, g_ref[...]).astype(BF16)
    acc = x
    for f in range(0, D_FF, FF_CHUNK):
        a = jnp.square(jnp.maximum(_dot(xn, w1_ref[:, f:f + FF_CHUNK]), 0.0))
        acc = acc + _dot(a.astype(BF16), w2_ref[f:f + FF_CHUNK, :])
    if final_norm:
        acc = _rms(acc, fg_ref[...])
    o_ref[...] = acc


def _mlp(h2, g, w1, w2, layer, fg, final_norm, proj=None):
    m = h2.shape[0]
    once = dict(pipeline_mode=pl.Buffered(1))
    in_specs = [pl.BlockSpec((ROW_TILE, D_MODEL), lambda i: (i, 0)),
                pl.BlockSpec((1, D_MODEL), lambda i: (0, 0)),
                pl.BlockSpec((None, D_MODEL, D_FF), lambda i: (layer, 0, 0), **once),
                pl.BlockSpec((None, D_FF, D_MODEL), lambda i: (layer, 0, 0), **once),
                pl.BlockSpec((1, D_MODEL), lambda i: (0, 0))]
    args = [h2, g.reshape(1, D_MODEL), w1, w2, fg.reshape(1, D_MODEL)]
    if proj is not None:
        a, wo, j = proj
        in_specs += [pl.BlockSpec((ROW_TILE, a.shape[1]), lambda i: (i, 0)),
                     pl.BlockSpec((None,) + wo.shape[1:], lambda i: (j, 0, 0), **once)]
        args += [a, wo]
    return pl.pallas_call(
        functools.partial(_mlp_kernel, final_norm=final_norm, with_proj=proj is not None),
        grid=(m // ROW_TILE,),
        in_specs=in_specs,
        out_specs=pl.BlockSpec((ROW_TILE, D_MODEL), lambda i: (i, 0)),
        out_shape=jax.ShapeDtypeStruct((m, D_MODEL), F32),
        compiler_params=_cparams(("parallel",)),
        name="mlp_final" if final_norm else "mlp",
    )(*args)


def _extra_scatter(sign_f, f_first):
    scat = np.zeros((LANES, N_HEADS * HEAD_PAD), np.float32)
    const = np.zeros((1, N_HEADS * HEAD_PAD), np.float32)
    for h in range(N_HEADS):
        base = h * HEAD_PAD + HEAD_DIM
        f0, o0 = (0, N_EXTRA) if f_first else (N_EXTRA, 0)
        for j in range(N_EXTRA):
            scat[j * N_HEADS + h, base + f0 + j] = sign_f
            const[0, base + o0 + j] = 1.0
    return jnp.asarray(scat, BF16), jnp.asarray(const, F32)


def _augment(proj, f_log2, sc_ref, cst_ref, out_ref):
    hi = f_log2.astype(BF16).astype(F32)
    r1 = f_log2 - hi
    mid = r1.astype(BF16).astype(F32)
    lo = (r1 - mid).astype(BF16).astype(F32)
    terms = hi + pltpu.roll(mid, N_HEADS, axis=1) + pltpu.roll(lo, 2 * N_HEADS, axis=1)
    extra = _dot(terms.astype(BF16), sc_ref[...]) + cst_ref[...]
    lane = lax.broadcasted_iota(jnp.int32, (proj.shape[0], LANES), 1)
    for hh in range(N_HEADS):
        blk = proj[:, (hh // 2) * LANES:(hh // 2 + 1) * LANES]
        if hh % 2:
            blk = pltpu.roll(blk, HEAD_DIM, axis=1)
        out_ref[0, hh] = jnp.where(lane < HEAD_DIM, blk,
                                   extra[:, hh * HEAD_PAD:(hh + 1) * HEAD_PAD]).astype(BF16)


def _kv_kernel(h_ref, g_ref, wk_ref, wv_ref, vone_ref, wf_ref, bf_ref, sc_ref, cst_ref,
               k_ref, v_ref, f_ref, carry_ref):
    @pl.when(pl.program_id(1) == 0)
    def _():
        carry_ref[...] = jnp.zeros_like(carry_ref)

    xn = _rms(h_ref[0], g_ref[...]).astype(BF16)
    v_ref[0, 0] = (_dot_nt(wv_ref[...], xn) + vone_ref[...]).astype(BF16)

    logit = _dot(xn, wf_ref[...]) + bf_ref[...]
    log_f = jnp.minimum(logit, 0.0) - jnp.log1p(jnp.exp(-jnp.abs(logit)))
    lane = lax.broadcasted_iota(jnp.int32, log_f.shape, 1)
    log_f = jnp.where(lane < N_HEADS, log_f, 0.0)
    t = log_f.shape[0]
    tri = (lax.broadcasted_iota(jnp.int32, (t, t), 0)
           >= lax.broadcasted_iota(jnp.int32, (t, t), 1)).astype(BF16)
    hi, mid, lo = _split3(log_f)
    cum = (_dot(tri, hi) + _dot(tri, mid)) + _dot(tri, lo) + carry_ref[...]
    carry_ref[...] = cum[t - 1:t, :]
    f_ref[0] = cum

    _augment(_dot(xn, wk_ref[...]), cum * LOG2E, sc_ref, cst_ref, k_ref)


def _kv_proj(h3, g, wk, wv, wf, bf):
    b, l, _ = h3.shape
    nt = l // ROW_TILE
    scat, const = _extra_scatter(-1.0, f_first=False)
    vone = np.zeros((N_HEADS, VT_ROWS, 1), np.float32)
    vone[:, HEAD_DIM] = 1.0
    vone = jnp.asarray(vone.reshape(N_HEADS * VT_ROWS, 1))
    full = lambda a: pl.BlockSpec(a.shape, lambda i, j: (0,) * a.ndim)
    return pl.pallas_call(
        _kv_kernel,
        grid=(b, nt),
        in_specs=[pl.BlockSpec((1, ROW_TILE, D_MODEL), lambda i, j: (i, j, 0)),
                  pl.BlockSpec((1, D_MODEL), lambda i, j: (0, 0)),
                  full(wk), full(wv), full(vone), full(wf), full(bf), full(scat), full(const)],
        out_specs=[pl.BlockSpec((1, N_HEADS, ROW_TILE, HEAD_PAD), lambda i, j: (i, 0, j, 0)),
                   pl.BlockSpec((1, 1, N_HEADS * VT_ROWS, ROW_TILE), lambda i, j: (i, j, 0, 0)),
                   pl.BlockSpec((1, ROW_TILE, LANES), lambda i, j: (i, j, 0))],
        out_shape=[jax.ShapeDtypeStruct((b, N_HEADS, l, HEAD_PAD), BF16),
                   jax.ShapeDtypeStruct((b, nt, N_HEADS * VT_ROWS, ROW_TILE), BF16),
                   jax.ShapeDtypeStruct((b, l, LANES), F32)],
        scratch_shapes=[pltpu.VMEM((1, LANES), F32)],
        compiler_params=_cparams(("parallel", "arbitrary")),
        name="kv_proj",
    )(h3, g.reshape(1, D_MODEL), wk, wv, vone, wf, bf, scat, const)


def _q_kernel(h_ref, g_ref, wq_ref, f_ref, sc_ref, cst_ref, q_ref):
    xn = _rms(h_ref[0], g_ref[...]).astype(BF16)
    _augment(_dot(xn, wq_ref[...]), f_ref[0] * LOG2E, sc_ref, cst_ref, q_ref)


def _q_proj(h3, g, wq, layer, fcum):
    b, l, _ = h3.shape
    scat, const = _extra_scatter(1.0, f_first=True)
    full = lambda a: pl.BlockSpec(a.shape, lambda i, j: (0,) * a.ndim)
    return pl.pallas_call(
        _q_kernel,
        grid=(b, l // ROW_TILE),
        in_specs=[pl.BlockSpec((1, ROW_TILE, D_MODEL), lambda i, j: (i, j, 0)),
                  pl.BlockSpec((1, D_MODEL), lambda i, j: (0, 0)),
                  pl.BlockSpec((None,) + wq.shape[1:], lambda i, j: (layer, 0, 0)),
                  pl.BlockSpec((1, ROW_TILE, LANES), lambda i, j: (i, j, 0)),
                  full(scat), full(const)],
        out_specs=pl.BlockSpec((1, N_HEADS, ROW_TILE, HEAD_PAD), lambda i, j: (i, 0, j, 0)),
        out_shape=jax.ShapeDtypeStruct((b, N_HEADS, l, HEAD_PAD), BF16),
        compiler_params=_cparams(("parallel", "parallel")),
        name="q_proj",
    )(h3, g.reshape(1, D_MODEL), wq, fcum, scat, const)


def _causal_items(length):
    return sum((qi * ATT_BQ) // ATT_BK + 1 for qi in range(length // ATT_BQ))


def _diag_bias():
    k = np.arange(ATT_BK)[:, None]
    q = np.arange(ATT_BQ)[None, :]
    tabs = [np.zeros((ATT_BK, ATT_BQ), np.float32)]
    for r in range(ATT_BK // ATT_BQ):
        tabs.append(np.where(k - q <= r * ATT_BQ, 0.0, NEG_BIG).astype(np.float32))
    return jnp.asarray(np.stack(tabs))


def _attn_kernel(q_ref, k_ref, vt_ref, bias_ref, o_ref, *scratch):
    s_scr = scratch[:ATT_SLOTS]
    p_scr = scratch[ATT_SLOTS:2 * ATT_SLOTS]
    acc_scr = scratch[2 * ATT_SLOTS]
    length = q_ref.shape[2]
    n_q = length // ATT_BQ
    n_items = _causal_items(length)
    assert n_items % ATT_SLOTS == 0

    def blocks_of(qi):
        return (qi * ATT_BQ) // ATT_BK + 1

    def advance(item):
        qi, j = item
        last = j + 1 == blocks_of(qi)
        return (jnp.minimum(jnp.where(last, qi + 1, qi), n_q - 1), jnp.where(last, 0, j + 1))

    def step(slot, item_s, item_x, item_v, chain, lane):
        m, acc = chain
        mb, alpha_q = lane
        out_chain, out_lane = [], []
        qi_v, j_v = item_v
        for hh in range(ATT_HEADS):
            vt = vt_ref[0, j_v, pl.ds(hh * VT_ROWS, VT_ROWS), :]
            acc_h = alpha_q[hh] * acc[hh] + _dot(vt, p_scr[slot][hh])
            acc_scr[qi_v, hh] = acc_h
            out_chain.append([None, acc_h])
        _, j_x = item_x
        for hh in range(ATT_HEADS):
            m_in = jnp.where(j_x == 0, NEG_BIG, m[hh])
            m_new = jnp.maximum(m_in, mb[hh])
            p_scr[slot][hh] = jnp.exp2(s_scr[slot][hh] - m_new).astype(BF16)
            out_chain[hh][0] = m_new
            out_lane.append([None, jnp.exp2(m_in - m_new)])
        qi_s, j_s = item_s
        shift = qi_s * ATT_BQ - j_s * ATT_BK
        sel = jnp.where(j_s + 1 == blocks_of(qi_s), 1 + shift // ATT_BQ, 0)
        for hh in range(ATT_HEADS):
            s = _dot_nt(k_ref[0, hh, pl.ds(j_s * ATT_BK, ATT_BK), :],
                        q_ref[0, hh, pl.ds(qi_s * ATT_BQ, ATT_BQ), :]) + bias_ref[sel]
            s_scr[slot][hh] = s
            out_lane[hh][0] = jnp.max(s, axis=0, keepdims=True)
        chain = tuple(tuple(c[i] for c in out_chain) for i in range(2))
        lane = tuple(tuple(c[i] for c in out_lane) for i in range(2))
        return chain, lane

    for slot in range(ATT_SLOTS):
        s_scr[slot][...] = jnp.full(s_scr[slot].shape, NULL_KEY, F32)
        p_scr[slot][...] = jnp.zeros(p_scr[slot].shape, BF16)

    def per_head(shape, v):
        pos = lax.broadcasted_iota(jnp.int32, shape, 0) + lax.broadcasted_iota(jnp.int32, shape, 1)
        return tuple(jnp.where(pos >= 0, v, 0.0).astype(F32) for _ in range(ATT_HEADS))

    row = (1, ATT_BQ)
    chain = (per_head(row, NEG_BIG), per_head((VT_ROWS, ATT_BQ), 0.0))
    lane = (per_head(row, NULL_KEY), per_head(row, 1.0))
    zero = jnp.int32(0)
    first = (zero, zero)

    def body(_, carry):
        item, hists, chain, lanes = carry
        new_hists, new_lanes = [], []
        for slot in range(ATT_SLOTS):
            item_x, item_v = hists[slot]
            chain, lane = step(slot, item, item_x, item_v, chain, lanes[slot])
            new_hists.append((item, item_x))
            new_lanes.append(lane)
            item = advance(item)
        return item, tuple(new_hists), chain, tuple(new_lanes)

    lax.fori_loop(0, n_items // ATT_SLOTS + 2, body,
                  (first, ((first, first),) * ATT_SLOTS, chain, (lane,) * ATT_SLOTS))

    def finish(qi, _):
        outs = []
        for hh in range(ATT_HEADS):
            a = acc_scr[qi, hh]
            outs.append(a[:HEAD_DIM] / a[HEAD_DIM:HEAD_DIM + 1])
        o_ref[0, pl.ds(qi * ATT_BQ, ATT_BQ), :] = jnp.concatenate(outs, axis=0).T.astype(BF16)
        return 0

    lax.fori_loop(0, n_q, finish, 0, unroll=FINISH_UNROLL)


def _attention(q_aug, k_aug, vt):
    b, _, l, _ = q_aug.shape
    bias = _diag_bias()
    return pl.pallas_call(
        _attn_kernel,
        grid=(b, N_HEADS // ATT_HEADS),
        in_specs=[pl.BlockSpec((1, ATT_HEADS, l, HEAD_PAD), lambda i, h: (i, h, 0, 0)),
                  pl.BlockSpec((1, ATT_HEADS, l, HEAD_PAD), lambda i, h: (i, h, 0, 0)),
                  pl.BlockSpec((1, l // ATT_BK, ATT_HEADS * VT_ROWS, ATT_BK), lambda i, h: (i, 0, h, 0)),
                  pl.BlockSpec(bias.shape, lambda i, h: (0, 0, 0))],
        out_specs=pl.BlockSpec((1, l, ATT_HEADS * HEAD_DIM), lambda i, h: (i, 0, h)),
        out_shape=jax.ShapeDtypeStruct((b, l, D_MODEL), BF16),
        scratch_shapes=[pltpu.VMEM((ATT_HEADS, ATT_BK, ATT_BQ), F32)] * ATT_SLOTS
                       + [pltpu.VMEM((ATT_HEADS, ATT_BK, ATT_BQ), BF16)] * ATT_SLOTS
                       + [pltpu.VMEM((l // ATT_BQ, ATT_HEADS, VT_ROWS, ATT_BQ), F32)],
        compiler_params=_cparams(("parallel", "parallel")),
        name="fox_attention",
    )(q_aug, k_aug, vt, bias)


def kernel(x, mix_norm, mlp_norm, mlp_w1, mlp_w2, ssm_log_dt, ssm_a_re, ssm_a_im,
           ssm_b_re, ssm_b_im, ssm_c_re, ssm_c_im, ssm_d, ssm_w_glu, kv_norm, w_kvf, b_f,
           attn_wq, attn_wo, final_norm):
    bsz, length, _ = x.shape
    m = bsz * length
    nk = length // SSM_CHUNK
    h = x.reshape(m, D_MODEL)
    w1_all, w2_all = mlp_w1.astype(BF16), mlp_w2.astype(BF16)
    wglu_all, wo_all = ssm_w_glu.astype(BF16), attn_wo.astype(BF16)
    wq_all = (attn_wq * (HEAD_DIM ** -0.5 * LOG2E)).astype(BF16)

    for i in range(DEPTH):
        if i < N_A_LAYERS:
            w2, pt, r, a1, a2 = _ssm_prep(ssm_log_dt[i], ssm_a_re[i], ssm_a_im[i], ssm_b_re[i],
                                          ssm_b_im[i], ssm_c_re[i], ssm_c_im[i])
            u = _norm_pack(h, mix_norm[i])
            v = _ssm_summary(u, r)
            s = _ssm_scan(v.reshape(bsz, nk, N_GROUPS, LANES),
                          a1.reshape(N_GROUPS, LANES), a2.reshape(N_GROUPS, LANES))
            d_t = jnp.tile(ssm_d[i].reshape(N_GROUPS, 1, SSM_GROUP), (1, 1, SSM_CHUNK))
            z = _ssm_output(u, s.reshape(bsz * nk, N_GROUPS * LANES), w2, pt, d_t)
            h = _glu(h, z, wglu_all, i)
            proj = None
        else:
            j = i - N_A_LAYERS
            q_aug = _q_proj(h.reshape(bsz, length, D_MODEL), mix_norm[i], wq_all, j, fcum)
            o = _attention(q_aug, k_aug, v_nat)
            proj = (o.reshape(m, D_MODEL), wo_all, j)
        h = _mlp(h, mlp_norm[i], w1_all, w2_all, i, final_norm,
                 final_norm=(i == DEPTH - 1), proj=proj)
        if i == N_A_LAYERS - 1:
            attn_dim = N_HEADS * HEAD_DIM
            wk = w_kvf[:, :attn_dim].astype(BF16)
            wv = w_kvf[:, attn_dim:2 * attn_dim].T.reshape(N_HEADS, HEAD_DIM, D_MODEL)
            wv = jnp.pad(wv, ((0, 0), (0, VT_ROWS - HEAD_DIM), (0, 0)))
            wv = wv.reshape(N_HEADS * VT_ROWS, D_MODEL).astype(BF16)
            wf = jnp.pad(w_kvf[:, 2 * attn_dim:], ((0, 0), (0, LANES - N_HEADS))).astype(BF16)
            bf = jnp.pad(b_f, (0, LANES - N_HEADS)).reshape(1, LANES)
            k_aug, v_nat, fcum = _kv_proj(h.reshape(bsz, length, D_MODEL), kv_norm,
                                          wk, wv, wf, bf)
    return h.reshape(bsz, length, D_MODEL)
```

```python
import functools
import math

import numpy as np
import jax
import jax.numpy as jnp
from jax import lax
from jax.experimental import pallas as pl
from jax.experimental.pallas import tpu as pltpu

D_MODEL = 1024
N_GROUPS = 64
SSM_GROUP = 16
GROUP_SHIFT = 4
SSM_STATE = 64
N_HEADS = 16
HEAD_DIM = 64
D_FF = 4 * D_MODEL
RMS_EPS = 1e-6
N_A_LAYERS = 2
DEPTH = 4

SSM_CHUNK = 64
CHUNK_W = SSM_CHUNK * SSM_GROUP
LANES = 128
HEAD_PAD = 128
N_EXTRA = 3

ROW_TILE = 512
FF_CHUNK = 1024
ATT_BQ = 256
VT_ROWS = 80
ATT_SLOTS = 8
SSM_GROUPS_PER_STEP = 4
GLU_SLAB = 2
FINISH_UNROLL = 4
ATT_HEADS = 2
ATT_BK = 512
NEG_BIG = -1e30
NULL_KEY = 3.0 * NEG_BIG
LOG2E = math.log2(math.e)

assert ROW_TILE == ATT_BK and ATT_BK % ATT_BQ == 0

VMEM_LIMIT = 56 * 1024 * 1024

F32 = jnp.float32
BF16 = jnp.bfloat16


def _cparams(sem):
    return pltpu.CompilerParams(dimension_semantics=sem, vmem_limit_bytes=VMEM_LIMIT)


def _rms(x, g):
    return x * lax.rsqrt(jnp.mean(x * x, axis=-1, keepdims=True) + RMS_EPS) * g


def _dot(a, b):
    return jnp.dot(a, b, preferred_element_type=F32)


def _dot_nt(a, b, precision=None):
    return lax.dot_general(a, b, (((1,), (1,)), ((), ())),
                           preferred_element_type=F32, precision=precision)


def _split3(x):
    hi = x.astype(BF16)
    r1 = x - hi.astype(F32)
    mid = r1.astype(BF16)
    lo = (r1 - mid.astype(F32)).astype(BF16)
    return hi, mid, lo


PACK_CHUNKS = 8
PIECES = LANES // SSM_GROUP
assert PACK_CHUNKS * SSM_CHUNK == ROW_TILE and PIECES == 8


def _piece_transpose(blocks):
    piece = lax.broadcasted_iota(jnp.int32, (PACK_CHUNKS, LANES), 1) >> GROUP_SHIFT
    for d in (4, 2, 1):
        keep = (piece & d) == 0
        nxt = []
        for v in blocks:
            out = list(v)
            for a in range(PIECES):
                if a & d:
                    continue
                b = a + d
                out[a] = jnp.where(keep, v[a], pltpu.roll(v[b], SSM_GROUP * d, axis=1))
                out[b] = jnp.where(keep, pltpu.roll(v[a], LANES - SSM_GROUP * d, axis=1), v[b])
            nxt.append(out)
        blocks = nxt
    return blocks


LANE_BLOCKS = D_MODEL // LANES


def _pack_chunks(tok_ref, grp_ref):
    n_sb = SSM_CHUNK // PIECES
    for gb in range(LANE_BLOCKS):
        v = [[tok_ref[gb, pl.ds(PIECES * sb + j, PACK_CHUNKS, stride=SSM_CHUNK), :]
              for j in range(PIECES)] for sb in range(n_sb)]
        w = _piece_transpose(v)
        for sb in range(n_sb):
            for i in range(PIECES):
                grp_ref[PIECES * gb + i, :, sb * LANES:(sb + 1) * LANES] = w[sb][i]


def _unpack_chunks(grp_ref, tok_ref, lane_blocks):
    n_sb = SSM_CHUNK // PIECES
    for gb in lane_blocks:
        w = [[grp_ref[PIECES * gb + i, :, sb * LANES:(sb + 1) * LANES] for i in range(PIECES)]
             for sb in range(n_sb)]
        v = _piece_transpose(w)
        for sb in range(n_sb):
            for j in range(PIECES):
                tok_ref[gb, pl.ds(PIECES * sb + j, PACK_CHUNKS, stride=SSM_CHUNK), :] = v[sb][j]


def _norm_pack_kernel(h_ref, g_ref, u_ref, xn_scr):
    xn = _rms(h_ref[...], g_ref[...])
    for gb in range(LANE_BLOCKS):
        xn_scr[gb] = xn[:, gb * LANES:(gb + 1) * LANES]
    _pack_chunks(xn_scr, u_ref)


def _norm_pack(h2, g):
    m = h2.shape[0]
    return pl.pallas_call(
        _norm_pack_kernel,
        grid=(m // ROW_TILE,),
        in_specs=[pl.BlockSpec((ROW_TILE, D_MODEL), lambda i: (i, 0)),
                  pl.BlockSpec((1, D_MODEL), lambda i: (0, 0))],
        out_specs=pl.BlockSpec((N_GROUPS, PACK_CHUNKS, CHUNK_W), lambda i: (0, i, 0)),
        out_shape=jax.ShapeDtypeStruct((N_GROUPS, m // SSM_CHUNK, CHUNK_W), F32),
        scratch_shapes=[pltpu.VMEM((LANE_BLOCKS, ROW_TILE, LANES), F32)],
        compiler_params=_cparams(("parallel",)),
        name="norm_pack",
    )(h2, g.reshape(1, D_MODEL))


def _swap_halves(x):
    return pltpu.roll(x, 64, axis=x.ndim - 1)


def _ssm_prep_kernel(ldt_ref, are_ref, aim_ref, bt_ref, cc_ref,
                     w2_ref, pt_ref, r_ref, a1_ref, a2_ref):
    lane1 = lax.broadcasted_iota(jnp.int32, (1, LANES), 1)
    lo1 = lane1 < SSM_STATE
    a_re = are_ref[0]
    a_im = aim_ref[0]
    dt = jnp.exp(ldt_ref[0])
    zr = a_re * dt
    zi = a_im * dt
    mag = jnp.exp(zr)
    lr = mag * jnp.cos(zi)
    li = mag * jnp.sin(zi)
    imag = jnp.exp(-zr)
    ir = imag * jnp.cos(zi)
    ii = -imag * jnp.sin(zi)

    sgn1 = jnp.where(lo1, -1.0, 1.0)

    def powers(count, nbits, wr, wi, descending=False):
        n = lax.broadcasted_iota(jnp.int32, (count, 1, LANES), 0)
        if descending:
            n = count - 1 - n
        lo3 = lax.broadcasted_iota(jnp.int32, (count, 1, LANES), 2) < SSM_STATE
        p = jnp.where(lo3, 1.0, 0.0).astype(F32)
        ps = jnp.where(lo3, 0.0, 1.0).astype(F32)
        for k in range(nbits):
            bit = ((n >> k) & 1) == 1
            wn = (sgn1 * wi)[None]
            p, ps = (jnp.where(bit, p * wr[None] + ps * wn, p),
                     jnp.where(bit, ps * wr[None] - p * wn, ps))
            wr, wi = wr * wr - wi * wi, 2.0 * wr * wi
        return p, ps

    def times_coef(table, coef, rows):
        p, ps = table
        cs = _swap_halves(coef)
        lo = lax.broadcasted_iota(jnp.int32, coef.shape, 1) < SSM_STATE
        prod = p * jnp.where(lo, coef, cs)[None] + ps * jnp.where(lo, -cs, coef)[None]
        return prod.reshape(rows, LANES)

    nr = lr - 1.0
    den = a_re * a_re + a_im * a_im
    cr = (nr * a_re + li * a_im) / den
    ci = (li * a_re - nr * a_im) / den
    bt = bt_ref[0]
    bbar = bt * cr + _swap_halves(bt) * (sgn1 * ci)
    cc = cc_ref[0]

    lo_big = lax.broadcasted_iota(jnp.int32, (CHUNK_W, LANES), 1) < SSM_STATE
    p_pos = powers(SSM_CHUNK, 6, lr, li)
    cbig = times_coef(p_pos, cc, CHUNK_W)
    wn = (sgn1 * li)[None]
    p_next = (p_pos[0] * lr[None] + p_pos[1] * wn, p_pos[1] * lr[None] - p_pos[0] * wn)
    cl1 = times_coef(p_next, cc, CHUNK_W)
    pt_ref[0] = jnp.where(lo_big, cl1, -cl1).astype(BF16)
    r_ref[0] = times_coef(powers(SSM_CHUNK, 6, lr, li, descending=True), bbar, CHUNK_W).astype(BF16)

    bsmall = times_coef(powers(LANES // SSM_GROUP, 3, ir, ii), bbar, LANES)
    lo8 = lax.broadcasted_iota(jnp.int32, (LANES, LANES), 1) < SSM_STATE
    lhs = jnp.where(lo8, bsmall, -bsmall)
    w2 = _dot_nt(lhs, cbig, precision=lax.Precision.HIGHEST)
    n8 = lax.broadcasted_iota(jnp.int32, (LANES, 1), 0) >> GROUP_SHIFT
    col_t = lax.broadcasted_iota(jnp.int32, (LANES, CHUNK_W), 1) >> GROUP_SHIFT
    w2_ref[0] = jnp.where(col_t >= n8, w2, 0.0).astype(BF16)

    ar, ai = lr, li
    for _ in range(6):
        ar, ai = ar * ar - ai * ai, 2.0 * ar * ai
    a1_ref[0] = ar
    a2_ref[0] = jnp.where(lo1, -ai, ai)


def _ssm_prep(log_dt, a_re, a_im, b_re, b_im, c_re, c_im):
    g = N_GROUPS
    dup = lambda x: jnp.concatenate([x, x], axis=-1).reshape(g, 1, LANES)
    ldt = jnp.broadcast_to(log_dt.reshape(g, 1, 1), (g, 1, LANES))
    bt = jnp.concatenate([b_re.transpose(0, 2, 1), b_im.transpose(0, 2, 1)], axis=-1)
    cc = jnp.concatenate([c_re, c_im], axis=-1)
    vec = pl.BlockSpec((1, 1, LANES), lambda i: (i, 0, 0))
    mat = pl.BlockSpec((1, SSM_GROUP, LANES), lambda i: (i, 0, 0))
    return pl.pallas_call(
        _ssm_prep_kernel,
        grid=(g,),
        in_specs=[vec, vec, vec, mat, mat],
        out_specs=[pl.BlockSpec((1, LANES, CHUNK_W), lambda i: (i, 0, 0)),
                   pl.BlockSpec((1, CHUNK_W, LANES), lambda i: (i, 0, 0)),
                   pl.BlockSpec((1, CHUNK_W, LANES), lambda i: (i, 0, 0)),
                   vec, vec],
        out_shape=[jax.ShapeDtypeStruct((g, LANES, CHUNK_W), BF16),
                   jax.ShapeDtypeStruct((g, CHUNK_W, LANES), BF16),
                   jax.ShapeDtypeStruct((g, CHUNK_W, LANES), BF16),
                   jax.ShapeDtypeStruct((g, 1, LANES), F32),
                   jax.ShapeDtypeStruct((g, 1, LANES), F32)],
        compiler_params=_cparams(("parallel",)),
        name="ssm_prep",
    )(ldt, dup(a_re), dup(a_im), bt, cc)


def _ssm_summary_kernel(u_ref, r_ref, v_ref):
    for gg in range(SSM_GROUPS_PER_STEP):
        v_ref[:, gg * LANES:(gg + 1) * LANES] = _dot(u_ref[gg].astype(BF16), r_ref[gg])


def _ssm_summary(u, r):
    g, m, _ = u.shape
    gs = SSM_GROUPS_PER_STEP
    return pl.pallas_call(
        _ssm_summary_kernel,
        grid=(g // gs,),
        in_specs=[pl.BlockSpec((gs, m, CHUNK_W), lambda i: (i, 0, 0)),
                  pl.BlockSpec((gs, CHUNK_W, LANES), lambda i: (i, 0, 0))],
        out_specs=pl.BlockSpec((m, gs * LANES), lambda i: (0, i)),
        out_shape=jax.ShapeDtypeStruct((m, g * LANES), F32),
        compiler_params=_cparams(("parallel",)),
        name="ssm_summary",
    )(u, r)


def _ssm_scan_kernel(v_ref, a1_ref, a2_ref, s_ref):
    a1 = a1_ref[...]
    a2 = a2_ref[...]
    nk = v_ref.shape[1]

    def step(k, s):
        s_ref[0, k] = s
        return s * a1 + _swap_halves(s) * a2 + v_ref[0, k]

    lax.fori_loop(0, nk, step, jnp.zeros((N_GROUPS, LANES), F32))


def _ssm_scan(v4, a1, a2):
    b, nk = v4.shape[0], v4.shape[1]
    blk = pl.BlockSpec((1, nk, N_GROUPS, LANES), lambda i: (i, 0, 0, 0))
    coef = pl.BlockSpec((N_GROUPS, LANES), lambda i: (0, 0))
    return pl.pallas_call(
        _ssm_scan_kernel,
        grid=(b,),
        in_specs=[blk, coef, coef],
        out_specs=blk,
        out_shape=jax.ShapeDtypeStruct(v4.shape, F32),
        compiler_params=_cparams(("parallel",)),
        name="ssm_scan",
    )(v4, a1, a2)


def _ssm_output_kernel(u_ref, s_ref, w2_ref, pt_ref, d_ref, z_ref, toep_ref):
    @pl.when(pl.program_id(0) == 0)
    def _():
        toep_ref[...] = jnp.zeros_like(toep_ref)

    nblk = CHUNK_W // LANES
    tile = 256
    c0 = math.sqrt(2.0 / math.pi)
    for gg in range(SSM_GROUPS_PER_STEP):
        for i in range(nblk):
            toep_ref[i * LANES:(i + 1) * LANES, i * LANES:] = w2_ref[gg, :, :CHUNK_W - i * LANES]
        u = u_ref[gg]
        ub = u.astype(BF16)
        y = _dot_nt(s_ref[:, gg * LANES:(gg + 1) * LANES].astype(BF16), pt_ref[gg]) + d_ref[gg] * u
        cols = []
        for j in range(CHUNK_W // tile):
            kk = (j + 1) * tile
            cols.append(_dot(ub[:, :kk], toep_ref[:kk, j * tile:(j + 1) * tile]))
        y = y + jnp.concatenate(cols, axis=1)
        z_ref[gg] = 0.5 * y * (1.0 + jnp.tanh(c0 * (y + 0.044715 * (y * y * y))))


def _ssm_output(u, s, w2, pt, d_t):
    g, m, _ = u.shape
    gs = SSM_GROUPS_PER_STEP
    return pl.pallas_call(
        _ssm_output_kernel,
        grid=(g // gs,),
        in_specs=[pl.BlockSpec((gs, m, CHUNK_W), lambda i: (i, 0, 0)),
                  pl.BlockSpec((m, gs * LANES), lambda i: (0, i)),
                  pl.BlockSpec((gs, LANES, CHUNK_W), lambda i: (i, 0, 0)),
                  pl.BlockSpec((gs, CHUNK_W, LANES), lambda i: (i, 0, 0)),
                  pl.BlockSpec((gs, 1, CHUNK_W), lambda i: (i, 0, 0))],
        out_specs=pl.BlockSpec((gs, m, CHUNK_W), lambda i: (i, 0, 0)),
        out_shape=jax.ShapeDtypeStruct((g, m, CHUNK_W), F32),
        scratch_shapes=[pltpu.VMEM((CHUNK_W, CHUNK_W), BF16)],
        compiler_params=_cparams(("arbitrary",)),
        name="ssm_output",
    )(u, s, w2, pt, d_t)


def _glu_kernel(h_ref, z_ref, w_ref, o_ref, z_scr):
    zw = None
    for first in range(0, LANE_BLOCKS, GLU_SLAB):
        blocks = tuple(range(first, first + GLU_SLAB))
        _unpack_chunks(z_ref, z_scr, blocks)
        z = jnp.concatenate([z_scr[gb].astype(BF16) for gb in blocks], axis=1)
        part = _dot(z, w_ref[first * LANES:(first + GLU_SLAB) * LANES, :])
        zw = part if zw is None else zw + part
    o_ref[...] = h_ref[...] + zw[:, :D_MODEL] * jax.nn.sigmoid(zw[:, D_MODEL:])


def _glu(h2, z, w, layer):
    m = h2.shape[0]
    return pl.pallas_call(
        _glu_kernel,
        grid=(m // ROW_TILE,),
        in_specs=[pl.BlockSpec((ROW_TILE, D_MODEL), lambda i: (i, 0)),
                  pl.BlockSpec((N_GROUPS, PACK_CHUNKS, CHUNK_W), lambda i: (0, i, 0)),
                  pl.BlockSpec((None,) + w.shape[1:], lambda i: (layer, 0, 0))],
        out_specs=pl.BlockSpec((ROW_TILE, D_MODEL), lambda i: (i, 0)),
        out_shape=jax.ShapeDtypeStruct((m, D_MODEL), F32),
        scratch_shapes=[pltpu.VMEM((LANE_BLOCKS, ROW_TILE, LANES), F32)],
        compiler_params=_cparams(("parallel",)),
        name="glu",
    )(h2, z, w)


def _mlp_kernel(h_ref, g_ref, w1_ref, w2_ref, fg_ref, *rest, final_norm, with_proj):
    if with_proj:
        a_ref, wo_ref, o_ref = rest
        x = h_ref[...] + _dot(a_ref[...], wo_ref[...])
    else:
        (o_ref,) = rest
        x = h_ref[...]
    xn = _rms(x, g_ref[...]).astype(BF16)
    acc = x
    for f in range(0, D_FF, FF_CHUNK):
        a = jnp.square(jnp.maximum(_dot(xn, w1_ref[:, f:f + FF_CHUNK]), 0.0))
        acc = acc + _dot(a.astype(BF16), w2_ref[f:f + FF_CHUNK, :])
    if final_norm:
        acc = _rms(acc, fg_ref[...])
    o_ref[...] = acc


def _mlp(h2, g, w1, w2, layer, fg, final_norm, proj=None):
    m = h2.shape[0]
    once = dict(pipeline_mode=pl.Buffered(1))
    in_specs = [pl.BlockSpec((ROW_TILE, D_MODEL), lambda i: (i, 0)),
                pl.BlockSpec((1, D_MODEL), lambda i: (0, 0)),
                pl.BlockSpec((None, D_MODEL, D_FF), lambda i: (layer, 0, 0), **once),
                pl.BlockSpec((None, D_FF, D_MODEL), lambda i: (layer, 0, 0), **once),
                pl.BlockSpec((1, D_MODEL), lambda i: (0, 0))]
    args = [h2, g.reshape(1, D_MODEL), w1, w2, fg.reshape(1, D_MODEL)]
    if proj is not None:
        a, wo, j = proj
        in_specs += [pl.BlockSpec((ROW_TILE, a.shape[1]), lambda i: (i, 0)),
                     pl.BlockSpec((None,) + wo.shape[1:], lambda i: (j, 0, 0), **once)]
        args += [a, wo]
    return pl.pallas_call(
        functools.partial(_mlp_kernel, final_norm=final_norm, with_proj=proj is not None),
        grid=(m // ROW_TILE,),
        in_specs=in_specs,
        out_specs=pl.BlockSpec((ROW_TILE, D_MODEL), lambda i: (i, 0)),
        out_shape=jax.ShapeDtypeStruct((m, D_MODEL), F32),
        compiler_params=_cparams(("parallel",)),
        name="mlp_final" if final_norm else "mlp",
    )(*args)


def _extra_scatter(sign_f, f_first):
    scat = np.zeros((LANES, N_HEADS * HEAD_PAD), np.float32)
    const = np.zeros((1, N_HEADS * HEAD_PAD), np.float32)
    for h in range(N_HEADS):
        base = h * HEAD_PAD + HEAD_DIM
        f0, o0 = (0, N_EXTRA) if f_first else (N_EXTRA, 0)
        for j in range(N_EXTRA):
            scat[j * N_HEADS + h, base + f0 + j] = sign_f
            const[0, base + o0 + j] = 1.0
    return jnp.asarray(scat, BF16), jnp.asarray(const, F32)


def _augment(proj, f_log2, sc_ref, cst_ref, out_ref):
    hi = f_log2.astype(BF16).astype(F32)
    r1 = f_log2 - hi
    mid = r1.astype(BF16).astype(F32)
    lo = (r1 - mid).astype(BF16).astype(F32)
    terms = hi + pltpu.roll(mid, N_HEADS, axis=1) + pltpu.roll(lo, 2 * N_HEADS, axis=1)
    extra = _dot(terms.astype(BF16), sc_ref[...]) + cst_ref[...]
    lane = lax.broadcasted_iota(jnp.int32, (proj.shape[0], LANES), 1)
    for hh in range(N_HEADS):
        blk = proj[:, (hh // 2) * LANES:(hh // 2 + 1) * LANES]
        if hh % 2:
            blk = pltpu.roll(blk, HEAD_DIM, axis=1)
        out_ref[0, hh] = jnp.where(lane < HEAD_DIM, blk,
                                   extra[:, hh * HEAD_PAD:(hh + 1) * HEAD_PAD]).astype(BF16)


def _kv_kernel(h_ref, g_ref, wk_ref, wv_ref, vone_ref, wf_ref, bf_ref, sc_ref, cst_ref,
               gq_ref, wq_ref, scq_ref, cstq_ref, k_ref, v_ref, f_ref, q_ref, carry_ref):
    @pl.when(pl.program_id(1) == 0)
    def _():
        carry_ref[...] = jnp.zeros_like(carry_ref)

    xn = _rms(h_ref[0], g_ref[...]).astype(BF16)
    v_ref[0, 0] = (_dot_nt(wv_ref[...], xn) + vone_ref[...]).astype(BF16)

    logit = _dot(xn, wf_ref[...]) + bf_ref[...]
    log_f = jnp.minimum(logit, 0.0) - jnp.log1p(jnp.exp(-jnp.abs(logit)))
    lane = lax.broadcasted_iota(jnp.int32, log_f.shape, 1)
    log_f = jnp.where(lane < N_HEADS, log_f, 0.0)
    t = log_f.shape[0]
    tri = (lax.broadcasted_iota(jnp.int32, (t, t), 0)
           >= lax.broadcasted_iota(jnp.int32, (t, t), 1)).astype(BF16)
    hi, mid, lo = _split3(log_f)
    cum = (_dot(tri, hi) + _dot(tri, mid)) + _dot(tri, lo) + carry_ref[...]
    carry_ref[...] = cum[t - 1:t, :]
    f_ref[0] = cum

    _augment(_dot(xn, wk_ref[...]), cum * LOG2E, sc_ref, cst_ref, k_ref)
    xq = _rms(h_ref[0], gq_ref[...]).astype(BF16)
    _augment(_dot(xq, wq_ref[...]), cum * LOG2E, scq_ref, cstq_ref, q_ref)


def _kv_proj(h3, g, wk, wv, wf, bf, gq, wq):
    b, l, _ = h3.shape
    nt = l // ROW_TILE
    scat, const = _extra_scatter(-1.0, f_first=False)
    scat_q, const_q = _extra_scatter(1.0, f_first=True)
    vone = np.zeros((N_HEADS, VT_ROWS, 1), np.float32)
    vone[:, HEAD_DIM] = 1.0
    vone = jnp.asarray(vone.reshape(N_HEADS * VT_ROWS, 1))
    full = lambda a: pl.BlockSpec(a.shape, lambda i, j: (0,) * a.ndim)
    return pl.pallas_call(
        _kv_kernel,
        grid=(b, nt),
        in_specs=[pl.BlockSpec((1, ROW_TILE, D_MODEL), lambda i, j: (i, j, 0)),
                  pl.BlockSpec((1, D_MODEL), lambda i, j: (0, 0)),
                  full(wk), full(wv), full(vone), full(wf), full(bf), full(scat), full(const),
                  pl.BlockSpec((1, D_MODEL), lambda i, j: (0, 0)),
                  pl.BlockSpec((None,) + wq.shape[1:], lambda i, j: (0, 0, 0)),
                  full(scat_q), full(const_q)],
        out_specs=[pl.BlockSpec((1, N_HEADS, ROW_TILE, HEAD_PAD), lambda i, j: (i, 0, j, 0)),
                   pl.BlockSpec((1, 1, N_HEADS * VT_ROWS, ROW_TILE), lambda i, j: (i, j, 0, 0)),
                   pl.BlockSpec((1, ROW_TILE, LANES), lambda i, j: (i, j, 0)),
                   pl.BlockSpec((1, N_HEADS, ROW_TILE, HEAD_PAD), lambda i, j: (i, 0, j, 0))],
        out_shape=[jax.ShapeDtypeStruct((b, N_HEADS, l, HEAD_PAD), BF16),
                   jax.ShapeDtypeStruct((b, nt, N_HEADS * VT_ROWS, ROW_TILE), BF16),
                   jax.ShapeDtypeStruct((b, l, LANES), F32),
                   jax.ShapeDtypeStruct((b, N_HEADS, l, HEAD_PAD), BF16)],
        scratch_shapes=[pltpu.VMEM((1, LANES), F32)],
        compiler_params=_cparams(("parallel", "arbitrary")),
        name="kv_proj",
    )(h3, g.reshape(1, D_MODEL), wk, wv, vone, wf, bf, scat, const,
      gq.reshape(1, D_MODEL), wq, scat_q, const_q)


def _q_kernel(h_ref, g_ref, wq_ref, f_ref, sc_ref, cst_ref, q_ref):
    xn = _rms(h_ref[0], g_ref[...]).astype(BF16)
    _augment(_dot(xn, wq_ref[...]), f_ref[0] * LOG2E, sc_ref, cst_ref, q_ref)


def _q_proj(h3, g, wq, layer, fcum):
    b, l, _ = h3.shape
    scat, const = _extra_scatter(1.0, f_first=True)
    full = lambda a: pl.BlockSpec(a.shape, lambda i, j: (0,) * a.ndim)
    return pl.pallas_call(
        _q_kernel,
        grid=(b, l // ROW_TILE),
        in_specs=[pl.BlockSpec((1, ROW_TILE, D_MODEL), lambda i, j: (i, j, 0)),
                  pl.BlockSpec((1, D_MODEL), lambda i, j: (0, 0)),
                  pl.BlockSpec((None,) + wq.shape[1:], lambda i, j: (layer, 0, 0)),
                  pl.BlockSpec((1, ROW_TILE, LANES), lambda i, j: (i, j, 0)),
                  full(scat), full(const)],
        out_specs=pl.BlockSpec((1, N_HEADS, ROW_TILE, HEAD_PAD), lambda i, j: (i, 0, j, 0)),
        out_shape=jax.ShapeDtypeStruct((b, N_HEADS, l, HEAD_PAD), BF16),
        compiler_params=_cparams(("parallel", "parallel")),
        name="q_proj",
    )(h3, g.reshape(1, D_MODEL), wq, fcum, scat, const)


def _causal_items(length):
    return sum((qi * ATT_BQ) // ATT_BK + 1 for qi in range(length // ATT_BQ))


def _diag_bias():
    k = np.arange(ATT_BK)[:, None]
    q = np.arange(ATT_BQ)[None, :]
    tabs = [np.zeros((ATT_BK, ATT_BQ), np.float32)]
    for r in range(ATT_BK // ATT_BQ):
        tabs.append(np.where(k - q <= r * ATT_BQ, 0.0, NEG_BIG).astype(np.float32))
    return jnp.asarray(np.stack(tabs))


def _attn_kernel(q_ref, k_ref, vt_ref, bias_ref, o_ref, *scratch):
    s_scr = scratch[:ATT_SLOTS]
    p_scr = scratch[ATT_SLOTS:2 * ATT_SLOTS]
    acc_scr = scratch[2 * ATT_SLOTS]
    length = q_ref.shape[2]
    n_q = length // ATT_BQ
    n_items = _causal_items(length)
    assert n_items % ATT_SLOTS == 0

    def blocks_of(qi):
        return (qi * ATT_BQ) // ATT_BK + 1

    def advance(item):
        qi, j = item
        last = j + 1 == blocks_of(qi)
        return (jnp.minimum(jnp.where(last, qi + 1, qi), n_q - 1), jnp.where(last, 0, j + 1))

    def step(slot, item_s, item_x, item_v, chain, lane):
        m, acc = chain
        mb, alpha_q = lane
        out_chain, out_lane = [], []
        qi_v, j_v = item_v
        for hh in range(ATT_HEADS):
            vt = vt_ref[0, j_v, pl.ds(hh * VT_ROWS, VT_ROWS), :]
            acc_h = alpha_q[hh] * acc[hh] + _dot(vt, p_scr[slot][hh])
            acc_scr[qi_v, hh] = acc_h
            out_chain.append([None, acc_h])
        _, j_x = item_x
        for hh in range(ATT_HEADS):
            m_in = jnp.where(j_x == 0, NEG_BIG, m[hh])
            m_new = jnp.maximum(m_in, mb[hh])
            p_scr[slot][hh] = jnp.exp2(s_scr[slot][hh] - m_new).astype(BF16)
            out_chain[hh][0] = m_new
            out_lane.append([None, jnp.exp2(m_in - m_new)])
        qi_s, j_s = item_s
        shift = qi_s * ATT_BQ - j_s * ATT_BK
        sel = jnp.where(j_s + 1 == blocks_of(qi_s), 1 + shift // ATT_BQ, 0)
        for hh in range(ATT_HEADS):
            s = _dot_nt(k_ref[0, hh, pl.ds(j_s * ATT_BK, ATT_BK), :],
                        q_ref[0, hh, pl.ds(qi_s * ATT_BQ, ATT_BQ), :]) + bias_ref[sel]
            s_scr[slot][hh] = s
            out_lane[hh][0] = jnp.max(s, axis=0, keepdims=True)
        chain = tuple(tuple(c[i] for c in out_chain) for i in range(2))
        lane = tuple(tuple(c[i] for c in out_lane) for i in range(2))
        return chain, lane

    for slot in range(ATT_SLOTS):
        s_scr[slot][...] = jnp.full(s_scr[slot].shape, NULL_KEY, F32)
        p_scr[slot][...] = jnp.zeros(p_scr[slot].shape, BF16)

    def per_head(shape, v):
        pos = lax.broadcasted_iota(jnp.int32, shape, 0) + lax.broadcasted_iota(jnp.int32, shape, 1)
        return tuple(jnp.where(pos >= 0, v, 0.0).astype(F32) for _ in range(ATT_HEADS))

    row = (1, ATT_BQ)
    chain = (per_head(row, NEG_BIG), per_head((VT_ROWS, ATT_BQ), 0.0))
    lane = (per_head(row, NULL_KEY), per_head(row, 1.0))
    zero = jnp.int32(0)
    first = (zero, zero)

    def body(_, carry):
        item, hists, chain, lanes = carry
        new_hists, new_lanes = [], []
        for slot in range(ATT_SLOTS):
            item_x, item_v = hists[slot]
            chain, lane = step(slot, item, item_x, item_v, chain, lanes[slot])
            new_hists.append((item, item_x))
            new_lanes.append(lane)
            item = advance(item)
        return item, tuple(new_hists), chain, tuple(new_lanes)

    lax.fori_loop(0, n_items // ATT_SLOTS + 2, body,
                  (first, ((first, first),) * ATT_SLOTS, chain, (lane,) * ATT_SLOTS))

    def finish(qi, _):
        outs = []
        for hh in range(ATT_HEADS):
            a = acc_scr[qi, hh]
            outs.append(a[:HEAD_DIM] / a[HEAD_DIM:HEAD_DIM + 1])
        o_ref[0, pl.ds(qi * ATT_BQ, ATT_BQ), :] = jnp.concatenate(outs, axis=0).T.astype(BF16)
        return 0

    lax.fori_loop(0, n_q, finish, 0, unroll=FINISH_UNROLL)


def _attention(q_aug, k_aug, vt):
    b, _, l, _ = q_aug.shape
    bias = _diag_bias()
    return pl.pallas_call(
        _attn_kernel,
        grid=(b, N_HEADS // ATT_HEADS),
        in_specs=[pl.BlockSpec((1, ATT_HEADS, l, HEAD_PAD), lambda i, h: (i, h, 0, 0)),
                  pl.BlockSpec((1, ATT_HEADS, l, HEAD_PAD), lambda i, h: (i, h, 0, 0)),
                  pl.BlockSpec((1, l // ATT_BK, ATT_HEADS * VT_ROWS, ATT_BK), lambda i, h: (i, 0, h, 0)),
                  pl.BlockSpec(bias.shape, lambda i, h: (0, 0, 0))],
        out_specs=pl.BlockSpec((1, l, ATT_HEADS * HEAD_DIM), lambda i, h: (i, 0, h)),
        out_shape=jax.ShapeDtypeStruct((b, l, D_MODEL), BF16),
        scratch_shapes=[pltpu.VMEM((ATT_HEADS, ATT_BK, ATT_BQ), F32)] * ATT_SLOTS
                       + [pltpu.VMEM((ATT_HEADS, ATT_BK, ATT_BQ), BF16)] * ATT_SLOTS
                       + [pltpu.VMEM((l // ATT_BQ, ATT_HEADS, VT_ROWS, ATT_BQ), F32)],
        compiler_params=_cparams(("parallel", "parallel")),
        name="fox_attention",
    )(q_aug, k_aug, vt, bias)


def kernel(x, mix_norm, mlp_norm, mlp_w1, mlp_w2, ssm_log_dt, ssm_a_re, ssm_a_im,
           ssm_b_re, ssm_b_im, ssm_c_re, ssm_c_im, ssm_d, ssm_w_glu, kv_norm, w_kvf, b_f,
           attn_wq, attn_wo, final_norm):
    bsz, length, _ = x.shape
    m = bsz * length
    nk = length // SSM_CHUNK
    h = x.reshape(m, D_MODEL)
    w1_all, w2_all = mlp_w1.astype(BF16), mlp_w2.astype(BF16)
    wglu_all, wo_all = ssm_w_glu.astype(BF16), attn_wo.astype(BF16)
    wq_all = (attn_wq * (HEAD_DIM ** -0.5 * LOG2E)).astype(BF16)

    for i in range(DEPTH):
        if i < N_A_LAYERS:
            w2, pt, r, a1, a2 = _ssm_prep(ssm_log_dt[i], ssm_a_re[i], ssm_a_im[i], ssm_b_re[i],
                                          ssm_b_im[i], ssm_c_re[i], ssm_c_im[i])
            u = _norm_pack(h, mix_norm[i])
            v = _ssm_summary(u, r)
            s = _ssm_scan(v.reshape(bsz, nk, N_GROUPS, LANES),
                          a1.reshape(N_GROUPS, LANES), a2.reshape(N_GROUPS, LANES))
            d_t = jnp.tile(ssm_d[i].reshape(N_GROUPS, 1, SSM_GROUP), (1, 1, SSM_CHUNK))
            z = _ssm_output(u, s.reshape(bsz * nk, N_GROUPS * LANES), w2, pt, d_t)
            h = _glu(h, z, wglu_all, i)
            proj = None
        else:
            j = i - N_A_LAYERS
            if j == 0:
                q_aug = q_first
            else:
                q_aug = _q_proj(h.reshape(bsz, length, D_MODEL), mix_norm[i], wq_all, j, fcum)
            o = _attention(q_aug, k_aug, v_nat)
            proj = (o.reshape(m, D_MODEL), wo_all, j)
        h = _mlp(h, mlp_norm[i], w1_all, w2_all, i, final_norm,
                 final_norm=(i == DEPTH - 1), proj=proj)
        if i == N_A_LAYERS - 1:
            attn_dim = N_HEADS * HEAD_DIM
            wk = w_kvf[:, :attn_dim].astype(BF16)
            wv = w_kvf[:, attn_dim:2 * attn_dim].T.reshape(N_HEADS, HEAD_DIM, D_MODEL)
            wv = jnp.pad(wv, ((0, 0), (0, VT_ROWS - HEAD_DIM), (0, 0)))
            wv = wv.reshape(N_HEADS * VT_ROWS, D_MODEL).astype(BF16)
            wf = jnp.pad(w_kvf[:, 2 * attn_dim:], ((0, 0), (0, LANES - N_HEADS))).astype(BF16)
            bf = jnp.pad(b_f, (0, LANES - N_HEADS)).reshape(1, LANES)
            k_aug, v_nat, fcum, q_first = _kv_proj(h.reshape(bsz, length, D_MODEL), kv_norm,
                                                   wk, wv, wf, bf, mix_norm[N_A_LAYERS], wq_all)
    return h.reshape(bsz, length, D_MODEL)
```
